```python
import jax, jax.numpy as jnp
from jax import lax
import numpy as np

D_MODEL = 1024
BATCH = 1
SEQ = 16384
DEPTH = 1
DEC_BATCH = 8
DEC_SEQ = 4096
PAST_LEN = 128

N_FOURIER_GROUPS = 4
FOURIER_GROUP_DIM = 128
FOURIER_WIDTH = N_FOURIER_GROUPS * FOURIER_GROUP_DIM
N_HEADS = 8
QK_NOPE_DIM = 64
QK_ROPE_DIM = 32
QK_HEAD_DIM = QK_NOPE_DIM + QK_ROPE_DIM
V_HEAD_DIM = 64
ATTN_WIDTH = N_HEADS * V_HEAD_DIM
Q_LORA_RANK = 384
KV_LORA_RANK = 256
ROPE_THETA = 10000.0
Q_BLOCK = 128
N_BRANCHES = 2
D_FF = -(-8 * D_MODEL // (3 * 256)) * 256
LN_EPS = 1e-5
RMS_EPS = 1e-6
DEEPNORM_ALPHA = (2.0 * DEPTH) ** 0.25
DEEPNORM_BETA = (8.0 * DEPTH) ** -0.25
IN_PROJ_DIM = FOURIER_WIDTH + Q_LORA_RANK + KV_LORA_RANK + QK_ROPE_DIM + N_BRANCHES * D_MODEL

kernel_name = "fnet_mla_gated_deepnorm_encoder"


def layer_norm(x, g, b):
    xf = x.astype(jnp.float32)
    mu = jnp.mean(xf, axis=-1, keepdims=True)
    var = jnp.mean(jnp.square(xf - mu), axis=-1, keepdims=True)
    y = (xf - mu) * lax.rsqrt(var + LN_EPS) * g.astype(jnp.float32) + b.astype(jnp.float32)
    return y.astype(x.dtype)


def rms_norm(x, g):
    xf = x.astype(jnp.float32)
    y = xf * lax.rsqrt(jnp.mean(jnp.square(xf), axis=-1, keepdims=True) + RMS_EPS) * g.astype(jnp.float32)
    return y.astype(x.dtype)


def rope(x, seq_len):
    half = QK_ROPE_DIM // 2
    inv_freq = 1.0 / (ROPE_THETA ** (jnp.arange(0, QK_ROPE_DIM, 2, dtype=jnp.float32) / QK_ROPE_DIM))
    ang = jnp.arange(seq_len, dtype=jnp.float32)[:, None] * inv_freq[None, :]
    cos = jnp.cos(ang)[None, :, None, :].astype(x.dtype)
    sin = jnp.sin(ang)[None, :, None, :].astype(x.dtype)
    x1, x2 = x[..., :half], x[..., half:]
    return jnp.concatenate([x1 * cos - x2 * sin, x2 * cos + x1 * sin], axis=-1)


def fourier_mix(u):
    b, s, _ = u.shape
    ug = u.astype(jnp.float32).reshape(b, s, N_FOURIER_GROUPS, FOURIER_GROUP_DIM)
    f = jnp.fft.fft2(ug, axes=(1, 3), norm="ortho").real
    return f.reshape(b, s, FOURIER_WIDTH).astype(u.dtype)


def mla_attention(c_q, c_kv, k_rope, g_q, w_uq, g_kv, w_ukv):
    b, s, _ = c_q.shape
    q = (rms_norm(c_q, g_q) @ w_uq).reshape(b, s, N_HEADS, QK_HEAD_DIM)
    q = jnp.concatenate([q[..., :QK_NOPE_DIM], rope(q[..., QK_NOPE_DIM:], s)], axis=-1)
    kv = (rms_norm(c_kv, g_kv) @ w_ukv).reshape(b, s, N_HEADS, QK_NOPE_DIM + V_HEAD_DIM)
    k_nope, v = kv[..., :QK_NOPE_DIM], kv[..., QK_NOPE_DIM:]
    k_pe = rope(k_rope[:, :, None, :], s)
    k = jnp.concatenate([k_nope, jnp.broadcast_to(k_pe, (b, s, N_HEADS, QK_ROPE_DIM))], axis=-1)
    scale = QK_HEAD_DIM ** -0.5
    n_blk = s // Q_BLOCK
    q_blocks = q.reshape(b, n_blk, Q_BLOCK, N_HEADS, QK_HEAD_DIM).transpose(1, 0, 2, 3, 4)

    def attend(qb):
        sc = jnp.einsum('bqhd,bkhd->bhqk', qb, k, preferred_element_type=jnp.float32) * scale
        p = jax.nn.softmax(sc, axis=-1).astype(v.dtype)
        return jnp.einsum('bhqk,bkhd->bqhd', p, v)

    o = lax.map(attend, q_blocks)
    return o.transpose(1, 0, 2, 3, 4).reshape(b, s, ATTN_WIDTH)


def encoder_layer(x, w_in, w_fourier, g_q, w_uq, g_kv, w_ukv, w_attn, w_o,
                  ln1_g, ln1_b, w_gate, w_up, w_down, ln2_g, ln2_b):
    b, s, _ = x.shape
    proj = x @ w_in
    o0 = FOURIER_WIDTH
    o1 = o0 + Q_LORA_RANK
    o2 = o1 + KV_LORA_RANK
    o3 = o2 + QK_ROPE_DIM
    u_f, c_q, c_kv, k_rope, gates = proj[..., :o0], proj[..., o0:o1], proj[..., o1:o2], proj[..., o2:o3], proj[..., o3:]
    branch_a = fourier_mix(u_f) @ w_fourier
    branch_b = mla_attention(c_q, c_kv, k_rope, g_q, w_uq, g_kv, w_ukv) @ w_attn
    g = jax.nn.sigmoid(gates.astype(jnp.float32)).astype(x.dtype).reshape(b, s, N_BRANCHES, D_MODEL)
    merged = g[:, :, 0] * branch_a + g[:, :, 1] * branch_b
    h = layer_norm(DEEPNORM_ALPHA * x + merged @ w_o, ln1_g, ln1_b)
    ffn = (jax.nn.silu(h @ w_gate) * (h @ w_up)) @ w_down
    return layer_norm(DEEPNORM_ALPHA * h + ffn, ln2_g, ln2_b)


def setup_inputs(seed: int = 0) -> dict:
    key = jax.random.key(seed)
    ks = jax.random.split(key, 18)
    f32 = jnp.float32

    def w(k, shape, fan_in, mult=1.0):
        return jax.random.normal(k, (DEPTH,) + shape, f32) * (fan_in ** -0.5) * mult

    def gain(k, n):
        return 1.0 + 0.02 * jax.random.normal(k, (DEPTH, n), f32)

    def bias(k, n):
        return 0.02 * jax.random.normal(k, (DEPTH, n), f32)

    return {
        "x_prompt": jax.random.normal(ks[0], (BATCH, SEQ, D_MODEL), f32),
        "x_sample": jax.random.normal(ks[1], (DEC_BATCH, DEC_SEQ, D_MODEL), f32),
        "w_in": w(ks[2], (D_MODEL, IN_PROJ_DIM), D_MODEL),
        "w_fourier": w(ks[3], (FOURIER_WIDTH, D_MODEL), FOURIER_WIDTH),
        "g_q": gain(ks[4], Q_LORA_RANK),
        "w_uq": w(ks[5], (Q_LORA_RANK, N_HEADS * QK_HEAD_DIM), Q_LORA_RANK),
        "g_kv": gain(ks[6], KV_LORA_RANK),
        "w_ukv": w(ks[7], (KV_LORA_RANK, N_HEADS * (QK_NOPE_DIM + V_HEAD_DIM)), KV_LORA_RANK),
        "w_attn": w(ks[8], (ATTN_WIDTH, D_MODEL), ATTN_WIDTH),
        "w_o": w(ks[9], (D_MODEL, D_MODEL), D_MODEL, DEEPNORM_BETA),
        "ln1_g": gain(ks[10], D_MODEL),
        "ln1_b": bias(ks[11], D_MODEL),
        "w_gate": w(ks[12], (D_MODEL, D_FF), D_MODEL),
        "w_up": w(ks[13], (D_MODEL, D_FF), D_MODEL),
        "w_down": w(ks[14], (D_FF, D_MODEL), D_FF, DEEPNORM_BETA),
        "ln2_g": gain(ks[15], D_MODEL),
        "ln2_b": bias(ks[16], D_MODEL),
    }


def _trunk(x, w_in, w_fourier, g_q, w_uq, g_kv, w_ukv, w_attn, w_o,
           ln1_g, ln1_b, w_gate, w_up, w_down, ln2_g, ln2_b):
    for l in range(DEPTH):
        x = encoder_layer(x, w_in[l], w_fourier[l], g_q[l], w_uq[l], g_kv[l], w_ukv[l], w_attn[l], w_o[l],
                          ln1_g[l], ln1_b[l], w_gate[l], w_up[l], w_down[l], ln2_g[l], ln2_b[l])
    return x


def reference(x_prompt, x_sample, w_in, w_fourier, g_q, w_uq, g_kv, w_ukv, w_attn, w_o,
              ln1_g, ln1_b, w_gate, w_up, w_down, ln2_g, ln2_b):
    y_prompt = _trunk(x_prompt, w_in, w_fourier, g_q, w_uq, g_kv, w_ukv, w_attn, w_o,
                      ln1_g, ln1_b, w_gate, w_up, w_down, ln2_g, ln2_b)
    y_sample = _trunk(x_sample, w_in, w_fourier, g_q, w_uq, g_kv, w_ukv, w_attn, w_o,
                      ln1_g, ln1_b, w_gate, w_up, w_down, ln2_g, ln2_b)
    return (y_prompt, y_sample)
```

```python
import functools
import math

import jax
import jax.numpy as jnp
from jax import lax
from jax.experimental import pallas as pl
from jax.experimental.pallas import tpu as pltpu

F32 = jnp.float32
BF16 = jnp.bfloat16

D_MODEL = 1024
N_GROUPS = 4
GROUP_DIM = 128
FOURIER_WIDTH = N_GROUPS * GROUP_DIM
N_HEADS = 8
NOPE_DIM = 64
ROPE_DIM = 32
QK_DIM = NOPE_DIM + ROPE_DIM
V_DIM = 64
ATTN_WIDTH = N_HEADS * V_DIM
Q_RANK = 384
KV_RANK = 256
ROPE_THETA = 10000.0
D_FF = 2816
LN_EPS = 1e-5
RMS_EPS = 1e-6

VMEM_BYTES_V7X = 64 * 1024 * 1024
LANES = 128

QK_PAD = 128
V_PAD = 80
DFT_N1 = 128
TOK_TILE = 512
DFT1_COLS = 4096
DFT2_GROUP = 8
ATTN_TQ = 512
ATTN_TK = 512
FFN_CHUNKS = 2


def _params(semantics, vmem_mb):
    return pltpu.CompilerParams(dimension_semantics=semantics, vmem_limit_bytes=vmem_mb * 1024 * 1024)


def _const_spec(shape):
    zeros = (0,) * len(shape)
    return pl.BlockSpec(shape, lambda *_: zeros, pipeline_mode=pl.Buffered(1))


def _dot(a, b):
    return jnp.dot(a, b, preferred_element_type=F32)


def _dot_nt(a, b):
    return lax.dot_general(a, b, (((1,), (1,)), ((), ())), preferred_element_type=F32)


def _dot_tn(a, b):
    return lax.dot_general(a, b, (((0,), (0,)), ((), ())), preferred_element_type=F32)


def _fold_kernel(wf_ref, cs_ref, out_ref):
    for g in range(N_GROUPS):
        w = wf_ref[:, g * GROUP_DIM:(g + 1) * GROUP_DIM]
        r = jnp.dot(w, cs_ref[...], preferred_element_type=F32, precision=lax.Precision.HIGHEST)
        out_ref[:, g * GROUP_DIM:(g + 1) * GROUP_DIM] = r[:, :GROUP_DIM].astype(BF16)
        out_ref[:, FOURIER_WIDTH + g * GROUP_DIM:FOURIER_WIDTH + (g + 1) * GROUP_DIM] = r[:, GROUP_DIM:].astype(BF16)


def _fold_call(w_f, cs):
    return pl.pallas_call(
        _fold_kernel,
        out_shape=jax.ShapeDtypeStruct((D_MODEL, 2 * FOURIER_WIDTH), BF16),
        name="fold_channel_dft",
    )(w_f, cs)


def _proj_kernel(x_ref, wab_ref, wq_ref, wkv_ref, wkr_ref, wkrr_ref, wg_ref, gq_ref, gkv_ref,
                 w1_ref, w2_ref, wuk_ref, wuvt_ref, cosq_ref, sinq_ref, cosk_ref, sink_ref,
                 ab_ref, qt_ref, k_ref, vt_ref, g_ref):
    xb = x_ref[0].astype(BF16)
    ab = _dot(xb, wab_ref[...])
    ab_ref[0, 0] = ab[:, :FOURIER_WIDTH].astype(BF16)
    ab_ref[0, 1] = ab[:, FOURIER_WIDTH:].astype(BF16)
    g_ref[0] = jax.nn.sigmoid(_dot(xb, wg_ref[...])).astype(BF16)

    cq = _dot(xb, wq_ref[...])
    cqn = (cq * lax.rsqrt(jnp.mean(jnp.square(cq), axis=-1, keepdims=True) + RMS_EPS) * gq_ref[...]).astype(BF16)
    ckv = _dot(xb, wkv_ref[...])
    ckvn = (ckv * lax.rsqrt(jnp.mean(jnp.square(ckv), axis=-1, keepdims=True) + RMS_EPS) * gkv_ref[...]).astype(BF16)

    kpe = _dot(xb, wkr_ref[...]) * cosk_ref[...] + _dot(xb, wkrr_ref[...]) * sink_ref[...]
    kn = _dot(ckvn, wuk_ref[...])
    for h in range(N_HEADS):
        k_ref[0, h] = (kn[:, h * QK_PAD:(h + 1) * QK_PAD] + kpe).astype(BF16)

    qt = _dot_nt(w1_ref[...], cqn)
    qr = _dot_nt(w2_ref[...], cqn)
    cos = cosq_ref[...]
    sin = sinq_ref[...]
    for h in range(N_HEADS):
        r0 = h * QK_PAD
        qt_ref[0, h, 0:NOPE_DIM, :] = qt[r0:r0 + NOPE_DIM].astype(BF16)
        qt_ref[0, h, NOPE_DIM:QK_DIM, :] = (
            qt[r0 + NOPE_DIM:r0 + QK_DIM] * cos + qr[h * ROPE_DIM:(h + 1) * ROPE_DIM] * sin).astype(BF16)
        qt_ref[0, h, QK_DIM:QK_PAD, :] = jnp.zeros((QK_PAD - QK_DIM, qt.shape[1]), BF16)

    vt = _dot_nt(wuvt_ref[...], ckvn)
    for h in range(N_HEADS):
        vt_ref[0, h, 0:V_DIM, :] = vt[h * V_DIM:(h + 1) * V_DIM].astype(BF16)
        vt_ref[0, h, V_DIM:V_PAD, :] = jnp.ones((V_PAD - V_DIM, vt.shape[1]), BF16)


def _proj_call(x, w, tabs):
    b, s, _ = x.shape
    tm = TOK_TILE
    consts = [w["wab"], w["wq"], w["wkv"], w["wkr"], w["wkrr"], w["wg"], w["gq"], w["gkv"],
              w["w1"], w["w2"], w["wuk"], w["wuvt"]]
    in_specs = [pl.BlockSpec((1, tm, D_MODEL), lambda bi, i: (bi, i, 0))]
    in_specs += [_const_spec(c.shape) for c in consts]
    in_specs += [
        pl.BlockSpec((ROPE_DIM, tm), lambda bi, i: (0, i)),
        pl.BlockSpec((ROPE_DIM, tm), lambda bi, i: (0, i)),
        pl.BlockSpec((tm, QK_PAD), lambda bi, i: (i, 0)),
        pl.BlockSpec((tm, QK_PAD), lambda bi, i: (i, 0)),
    ]
    out_shape = (
        jax.ShapeDtypeStruct((b, 2, s, FOURIER_WIDTH), BF16),
        jax.ShapeDtypeStruct((b, N_HEADS, QK_PAD, s), BF16),
        jax.ShapeDtypeStruct((b, N_HEADS, s, QK_PAD), BF16),
        jax.ShapeDtypeStruct((b, N_HEADS, V_PAD, s), BF16),
        jax.ShapeDtypeStruct((b, s, 2 * D_MODEL), BF16),
    )
    out_specs = (
        pl.BlockSpec((1, 2, tm, FOURIER_WIDTH), lambda bi, i: (bi, 0, i, 0)),
        pl.BlockSpec((1, N_HEADS, QK_PAD, tm), lambda bi, i: (bi, 0, 0, i)),
        pl.BlockSpec((1, N_HEADS, tm, QK_PAD), lambda bi, i: (bi, 0, i, 0)),
        pl.BlockSpec((1, N_HEADS, V_PAD, tm), lambda bi, i: (bi, 0, 0, i)),
        pl.BlockSpec((1, tm, 2 * D_MODEL), lambda bi, i: (bi, i, 0)),
    )
    return pl.pallas_call(
        _proj_kernel,
        grid=(b, s // tm),
        in_specs=in_specs,
        out_specs=out_specs,
        out_shape=out_shape,
        compiler_params=_params(("parallel", "parallel"), 56),
        name="proj",
    )(x, *consts, tabs["cosq"], tabs["sinq"], tabs["cosk"], tabs["sink"])


def _dft1_kernel(m1_ref, ab_ref, y_ref):
    n1, tc = ab_ref.shape[2], ab_ref.shape[3]
    z = ab_ref[0].reshape(2 * n1, tc)
    y = _dot(m1_ref[...], z)
    y_ref[0] = y.astype(BF16).reshape(2, n1, tc)


def _dft1_call(ab, m1):
    b, _, n1, cols = ab.shape
    tc = min(DFT1_COLS, cols)
    return pl.pallas_call(
        _dft1_kernel,
        grid=(b, cols // tc),
        in_specs=[_const_spec(m1.shape), pl.BlockSpec((1, 2, n1, tc), lambda bi, i: (bi, 0, 0, i))],
        out_specs=pl.BlockSpec((1, 2, n1, tc), lambda bi, i: (bi, 0, 0, i)),
        out_shape=jax.ShapeDtypeStruct(ab.shape, BF16),
        compiler_params=_params(("parallel", "parallel"), 32),
        name="dft_stage1",
    )(m1, ab)


def _dft2_kernel(t_ref, y_ref, x_ref):
    n2 = y_ref.shape[3]
    for g in range(DFT2_GROUP):
        yk = jnp.concatenate([y_ref[0, 0, g], y_ref[0, 1, g]], axis=0)
        x_ref[0, g] = _dot(t_ref[g], yk).astype(BF16)


def _dft2_call(y, t):
    b, _, n1, n2, fw = y.shape
    g = DFT2_GROUP
    return pl.pallas_call(
        _dft2_kernel,
        grid=(b, n1 // g),
        in_specs=[pl.BlockSpec((g, n2, 2 * n2), lambda bi, i: (i, 0, 0)),
                  pl.BlockSpec((1, 2, g, n2, fw), lambda bi, i: (bi, 0, i, 0, 0))],
        out_specs=pl.BlockSpec((1, g, n2, fw), lambda bi, i: (bi, i, 0, 0)),
        out_shape=jax.ShapeDtypeStruct((b, n1, n2, fw), BF16),
        compiler_params=_params(("parallel", "parallel"), 32),
        name="dft_stage2",
    )(t, y)


def _attn_kernel(qt_ref, k_ref, vt_ref, o_ref, m_ref, acc_ref):
    s_len = k_ref.shape[2]
    tk = ATTN_TK
    qt = qt_ref[0, 0]
    m_ref[...] = jnp.full(m_ref.shape, -1e30, F32)
    acc_ref[...] = jnp.zeros(acc_ref.shape, F32)

    def body(j, carry):
        off = pl.multiple_of(j * tk, tk)
        kb = k_ref[0, 0, pl.ds(off, tk), :]
        sc = _dot(kb, qt)
        m_old = m_ref[...]
        m_new = jnp.maximum(m_old, jnp.max(sc, axis=0, keepdims=True))
        p = jnp.exp2(sc - m_new).astype(BF16)
        vb = vt_ref[0, 0, :, pl.ds(off, tk)]
        acc_ref[...] = acc_ref[...] * jnp.exp2(m_old - m_new) + _dot(vb, p)
        m_ref[...] = m_new
        return carry

    lax.fori_loop(0, s_len // tk, body, 0)
    acc = acc_ref[...]
    o_ref[0] = (acc[:V_DIM] / acc[V_DIM:V_DIM + 1]).astype(BF16)


def _attn_call(qt, k, vt):
    b, h, _, s = qt.shape
    tq = ATTN_TQ
    return pl.pallas_call(
        _attn_kernel,
        grid=(b, h, s // tq),
        in_specs=[pl.BlockSpec((1, 1, QK_PAD, tq), lambda bi, hi, i: (bi, hi, 0, i)),
                  pl.BlockSpec((1, 1, s, QK_PAD), lambda bi, hi, i: (bi, hi, 0, 0)),
                  pl.BlockSpec((1, 1, V_PAD, s), lambda bi, hi, i: (bi, hi, 0, 0))],
        out_specs=pl.BlockSpec((1, V_DIM, tq), lambda bi, hi, i: (bi, hi, i)),
        out_shape=jax.ShapeDtypeStruct((b, ATTN_WIDTH, s), BF16),
        scratch_shapes=[pltpu.VMEM((1, tq), F32), pltpu.VMEM((V_PAD, tq), F32)],
        compiler_params=_params(("parallel", "parallel", "arbitrary"), 48),
        name="attention",
    )(qt, k, vt)


def _layer_norm(v, g, b):
    mu = jnp.mean(v, axis=-1, keepdims=True)
    var = jnp.mean(jnp.square(v - mu), axis=-1, keepdims=True)
    return (v - mu) * lax.rsqrt(var + LN_EPS) * g + b


def _final_kernel(alpha, n_f, x_ref, g_ref, *rest):
    f_refs = rest[:n_f]
    (ot_ref, wf_ref, wa_ref, wo_ref, wgate_ref, wup_ref, wdown_ref,
     ln1g_ref, ln1b_ref, ln2g_ref, ln2b_ref, y_ref) = rest[n_f:]
    fa = jnp.concatenate([r[0] for r in f_refs], axis=0)
    branch_a = _dot(fa, wf_ref[...])
    branch_b = _dot_tn(ot_ref[0], wa_ref[...])
    g = g_ref[0]
    merged = (g[:, :D_MODEL].astype(F32) * branch_a + g[:, D_MODEL:].astype(F32) * branch_b).astype(BF16)
    h = _layer_norm(alpha * x_ref[0] + _dot(merged, wo_ref[...]), ln1g_ref[...], ln1b_ref[...])
    hb = h.astype(BF16)
    ck = D_FF // FFN_CHUNKS
    ffn = None
    for c in range(FFN_CHUNKS):
        gate = _dot(hb, wgate_ref[:, c * ck:(c + 1) * ck])
        up = _dot(hb, wup_ref[:, c * ck:(c + 1) * ck])
        part = _dot((jax.nn.silu(gate) * up).astype(BF16), wdown_ref[c * ck:(c + 1) * ck, :])
        ffn = part if ffn is None else ffn + part
    y_ref[0] = _layer_norm(alpha * h + ffn, ln2g_ref[...], ln2b_ref[...])


def _final_call(x, gates, x3, ot, w, alpha):
    b, s, _ = x.shape
    tm = TOK_TILE
    n_f = tm // DFT_N1
    consts = [w["wf"], w["wa"], w["wo"], w["wgate"], w["wup"], w["wdown"],
              w["ln1g"], w["ln1b"], w["ln2g"], w["ln2b"]]
    in_specs = [pl.BlockSpec((1, tm, D_MODEL), lambda bi, i: (bi, i, 0)),
                pl.BlockSpec((1, tm, 2 * D_MODEL), lambda bi, i: (bi, i, 0))]
    in_specs += [pl.BlockSpec((1, DFT_N1, FOURIER_WIDTH), functools.partial(lambda j, bi, i: (bi, 0, n_f * i + j), j))
                 for j in range(n_f)]
    in_specs += [pl.BlockSpec((1, ATTN_WIDTH, tm), lambda bi, i: (bi, 0, i))]
    in_specs += [_const_spec(c.shape) for c in consts]
    return pl.pallas_call(
        functools.partial(_final_kernel, alpha, n_f),
        grid=(b, s // tm),
        in_specs=in_specs,
        out_specs=pl.BlockSpec((1, tm, D_MODEL), lambda bi, i: (bi, i, 0)),
        out_shape=jax.ShapeDtypeStruct((b, s, D_MODEL), F32),
        compiler_params=_params(("parallel", "parallel"), 56),
        name="merge_ffn",
    )(x, gates, *([x3] * n_f), ot, *consts)


def _cos_sin(num, den):
    ang = (2.0 * math.pi / den) * (num % den).astype(F32)
    return jnp.cos(ang), jnp.sin(ang)


def _channel_dft_table():
    j = lax.broadcasted_iota(jnp.int32, (GROUP_DIM, GROUP_DIM), 0)
    k = lax.broadcasted_iota(jnp.int32, (GROUP_DIM, GROUP_DIM), 1)
    c, s = _cos_sin(j * k, GROUP_DIM)
    return jnp.concatenate([c, s], axis=1) * (GROUP_DIM ** -0.5)


def _dft_tables(s_len):
    n1, n2 = DFT_N1, s_len // DFT_N1
    k1 = lax.broadcasted_iota(jnp.int32, (n1, n1), 0)
    i1 = lax.broadcasted_iota(jnp.int32, (n1, n1), 1)
    c, s = _cos_sin(k1 * i1, n1)
    m1 = jnp.block([[c, -s], [-s, -c]]) * (n1 ** -0.5)
    shape = (n1, n2, n2)
    k = lax.broadcasted_iota(jnp.int32, shape, 0) + n1 * lax.broadcasted_iota(jnp.int32, shape, 1)
    i2 = lax.broadcasted_iota(jnp.int32, shape, 2)
    tc, ts = _cos_sin(k * i2, s_len)
    t = jnp.concatenate([tc, ts], axis=2) * (n2 ** -0.5)
    return m1.astype(BF16), t.astype(BF16)


def _rope_tables(s_len):
    half = ROPE_DIM // 2
    inv_freq = 1.0 / (ROPE_THETA ** (jnp.arange(0, ROPE_DIM, 2, dtype=F32) / ROPE_DIM))
    ang = jnp.arange(s_len, dtype=F32)[:, None] * inv_freq[None, :]
    cos2 = jnp.concatenate([jnp.cos(ang)] * 2, axis=1)
    sin2 = jnp.concatenate([jnp.sin(ang)] * 2, axis=1)
    pad = lambda t: jnp.pad(t, ((0, 0), (NOPE_DIM, QK_PAD - QK_DIM)))
    del half
    return {"cosq": cos2.T, "sinq": sin2.T, "cosk": pad(cos2), "sink": pad(sin2)}


def _rotate_half_cols(w):
    half = ROPE_DIM // 2
    return jnp.concatenate([-w[..., half:], w[..., :half]], axis=-1)


def _layer_weights(w_in, w_fourier, g_q, w_uq, g_kv, w_ukv, w_attn, w_o,
                   ln1_g, ln1_b, w_gate, w_up, w_down, ln2_g, ln2_b):
    o0 = FOURIER_WIDTH
    o1 = o0 + Q_RANK
    o2 = o1 + KV_RANK
    o3 = o2 + ROPE_DIM
    lane_pad = lambda t: jnp.pad(t, ((0, 0), (NOPE_DIM, QK_PAD - QK_DIM)))
    wkr = w_in[:, o2:o3]
    q_scale = (QK_DIM ** -0.5) * math.log2(math.e)
    uq = (w_uq * q_scale).reshape(Q_RANK, N_HEADS, QK_DIM)
    w1 = jnp.pad(uq, ((0, 0), (0, 0), (0, QK_PAD - QK_DIM))).reshape(Q_RANK, N_HEADS * QK_PAD).T
    w2 = _rotate_half_cols(uq[:, :, NOPE_DIM:]).reshape(Q_RANK, N_HEADS * ROPE_DIM).T
    ukv = w_ukv.reshape(KV_RANK, N_HEADS, NOPE_DIM + V_DIM)
    wuk = jnp.pad(ukv[:, :, :NOPE_DIM], ((0, 0), (0, 0), (0, QK_PAD - NOPE_DIM))).reshape(KV_RANK, N_HEADS * QK_PAD)
    wuvt = ukv[:, :, NOPE_DIM:].reshape(KV_RANK, ATTN_WIDTH).T
    row = lambda v: v.reshape(1, -1).astype(F32)
    return {
        "wab": _fold_call(w_in[:, :o0], _channel_dft_table()),
        "wq": w_in[:, o0:o1].astype(BF16), "wkv": w_in[:, o1:o2].astype(BF16),
        "wkr": lane_pad(wkr).astype(BF16), "wkrr": lane_pad(_rotate_half_cols(wkr)).astype(BF16),
        "wg": w_in[:, o3:].astype(BF16), "gq": row(g_q), "gkv": row(g_kv),
        "w1": w1.astype(BF16), "w2": w2.astype(BF16), "wuk": wuk.astype(BF16), "wuvt": wuvt.astype(BF16),
        "wf": w_fourier.astype(BF16), "wa": w_attn.astype(BF16), "wo": w_o.astype(BF16),
        "wgate": w_gate.astype(BF16), "wup": w_up.astype(BF16), "wdown": w_down.astype(BF16),
        "ln1g": row(ln1_g), "ln1b": row(ln1_b), "ln2g": row(ln2_g), "ln2b": row(ln2_b),
    }


def _encoder_layer(x, w, alpha):
    b, s, _ = x.shape
    n1, n2 = DFT_N1, s // DFT_N1
    m1, t = _dft_tables(s)
    ab, qt, k, vt, gates = _proj_call(x, w, _rope_tables(s))
    y = _dft1_call(ab.reshape(b, 2, n1, n2 * FOURIER_WIDTH), m1)
    x3 = _dft2_call(y.reshape(b, 2, n1, n2, FOURIER_WIDTH), t)
    ot = _attn_call(qt, k, vt)
    return _final_call(x, gates, x3.reshape(b, n1, n2 * FOURIER_WIDTH), ot, w, alpha)


def kernel(x_prompt, x_sample, w_in, w_fourier, g_q, w_uq, g_kv, w_ukv, w_attn, w_o, ln1_g, ln1_b, w_gate, w_up, w_down, ln2_g, ln2_b):
    stacked = (w_in, w_fourier, g_q, w_uq, g_kv, w_ukv, w_attn, w_o, ln1_g, ln1_b, w_gate, w_up, w_down, ln2_g, ln2_b)
    depth = w_in.shape[0]
    alpha = (2.0 * depth) ** 0.25
    layers = [_layer_weights(*(t[l] for t in stacked)) for l in range(depth)]
    outs = []
    for x in (x_prompt, x_sample):
        for w in layers:
            x = _encoder_layer(x, w, alpha)
        outs.append(x)
    return tuple(outs)
```

```python
import functools
import math

import jax
import jax.numpy as jnp
from jax import lax
from jax.experimental import pallas as pl
from jax.experimental.pallas import tpu as pltpu

F32 = jnp.float32
BF16 = jnp.bfloat16

D_MODEL = 1024
N_GROUPS = 4
GROUP_DIM = 128
FOURIER_WIDTH = N_GROUPS * GROUP_DIM
N_HEADS = 8
NOPE_DIM = 64
ROPE_DIM = 32
QK_DIM = NOPE_DIM + ROPE_DIM
V_DIM = 64
ATTN_WIDTH = N_HEADS * V_DIM
Q_RANK = 384
KV_RANK = 256
ROPE_THETA = 10000.0
D_FF = 2816
LN_EPS = 1e-5
RMS_EPS = 1e-6

VMEM_BYTES_V7X = 64 * 1024 * 1024
LANES = 128

QK_PAD = 128
V_PAD = 80
DFT_N1 = 128
TOK_TILE = 512
DFT1_COLS = 4096
DFT2_GROUP = 8
ATTN_TQ = 512
ATTN_TK = 512
ATTN_UNROLL = 4
FFN_CHUNKS = 2


def _params(semantics, vmem_mb):
    return pltpu.CompilerParams(dimension_semantics=semantics, vmem_limit_bytes=vmem_mb * 1024 * 1024)


def _const_spec(shape):
    zeros = (0,) * len(shape)
    return pl.BlockSpec(shape, lambda *_: zeros, pipeline_mode=pl.Buffered(1))


def _dot(a, b):
    return jnp.dot(a, b, preferred_element_type=F32)


def _dot_nt(a, b):
    return lax.dot_general(a, b, (((1,), (1,)), ((), ())), preferred_element_type=F32)


def _dot_tn(a, b):
    return lax.dot_general(a, b, (((0,), (0,)), ((), ())), preferred_element_type=F32)


def _fold_kernel(wf_ref, cs_ref, out_ref):
    for g in range(N_GROUPS):
        w = wf_ref[:, g * GROUP_DIM:(g + 1) * GROUP_DIM]
        r = jnp.dot(w, cs_ref[...], preferred_element_type=F32, precision=lax.Precision.HIGHEST)
        out_ref[:, g * GROUP_DIM:(g + 1) * GROUP_DIM] = r[:, :GROUP_DIM].astype(BF16)
        out_ref[:, FOURIER_WIDTH + g * GROUP_DIM:FOURIER_WIDTH + (g + 1) * GROUP_DIM] = r[:, GROUP_DIM:].astype(BF16)


def _fold_call(w_f, cs):
    return pl.pallas_call(
        _fold_kernel,
        out_shape=jax.ShapeDtypeStruct((D_MODEL, 2 * FOURIER_WIDTH), BF16),
        name="fold_channel_dft",
    )(w_f, cs)


def _proj_kernel(x_ref, wab_ref, wq_ref, wkv_ref, wkr_ref, wkrr_ref, wg_ref, gq_ref, gkv_ref,
                 w1_ref, w2_ref, wuk_ref, wuvt_ref, cosq_ref, sinq_ref, cosk_ref, sink_ref,
                 ab_ref, qt_ref, k_ref, vt_ref, g_ref):
    xb = x_ref[0].astype(BF16)
    ab = _dot(xb, wab_ref[...])
    ab_ref[0, 0] = ab[:, :FOURIER_WIDTH].astype(BF16)
    ab_ref[0, 1] = ab[:, FOURIER_WIDTH:].astype(BF16)
    g_ref[0] = jax.nn.sigmoid(_dot(xb, wg_ref[...])).astype(BF16)

    cq = _dot(xb, wq_ref[...])
    cqn = (cq * lax.rsqrt(jnp.mean(jnp.square(cq), axis=-1, keepdims=True) + RMS_EPS) * gq_ref[...]).astype(BF16)
    ckv = _dot(xb, wkv_ref[...])
    ckvn = (ckv * lax.rsqrt(jnp.mean(jnp.square(ckv), axis=-1, keepdims=True) + RMS_EPS) * gkv_ref[...]).astype(BF16)

    kpe = _dot(xb, wkr_ref[...]) * cosk_ref[...] + _dot(xb, wkrr_ref[...]) * sink_ref[...]
    kn = _dot(ckvn, wuk_ref[...])
    for h in range(N_HEADS):
        k_ref[0, h] = (kn[:, h * QK_PAD:(h + 1) * QK_PAD] + kpe).astype(BF16)

    qt = _dot_nt(w1_ref[...], cqn)
    qr = _dot_nt(w2_ref[...], cqn)
    cos = cosq_ref[...]
    sin = sinq_ref[...]
    for h in range(N_HEADS):
        r0 = h * QK_PAD
        qt_ref[0, h, 0:NOPE_DIM, :] = qt[r0:r0 + NOPE_DIM].astype(BF16)
        qt_ref[0, h, NOPE_DIM:QK_DIM, :] = (
            qt[r0 + NOPE_DIM:r0 + QK_DIM] * cos + qr[h * ROPE_DIM:(h + 1) * ROPE_DIM] * sin).astype(BF16)
        qt_ref[0, h, QK_DIM:QK_PAD, :] = jnp.zeros((QK_PAD - QK_DIM, qt.shape[1]), BF16)

    vt = _dot_nt(wuvt_ref[...], ckvn)
    for h in range(N_HEADS):
        vt_ref[0, h, 0:V_DIM, :] = vt[h * V_DIM:(h + 1) * V_DIM].astype(BF16)
        vt_ref[0, h, V_DIM:V_PAD, :] = jnp.ones((V_PAD - V_DIM, vt.shape[1]), BF16)


def _proj_call(x, w, tabs):
    b, s, _ = x.shape
    tm = TOK_TILE
    consts = [w["wab"], w["wq"], w["wkv"], w["wkr"], w["wkrr"], w["wg"], w["gq"], w["gkv"],
              w["w1"], w["w2"], w["wuk"], w["wuvt"]]
    in_specs = [pl.BlockSpec((1, tm, D_MODEL), lambda bi, i: (bi, i, 0))]
    in_specs += [_const_spec(c.shape) for c in consts]
    in_specs += [
        pl.BlockSpec((ROPE_DIM, tm), lambda bi, i: (0, i)),
        pl.BlockSpec((ROPE_DIM, tm), lambda bi, i: (0, i)),
        pl.BlockSpec((tm, QK_PAD), lambda bi, i: (i, 0)),
        pl.BlockSpec((tm, QK_PAD), lambda bi, i: (i, 0)),
    ]
    out_shape = (
        jax.ShapeDtypeStruct((b, 2, s, FOURIER_WIDTH), BF16),
        jax.ShapeDtypeStruct((b, N_HEADS, QK_PAD, s), BF16),
        jax.ShapeDtypeStruct((b, N_HEADS, s, QK_PAD), BF16),
        jax.ShapeDtypeStruct((b, N_HEADS, V_PAD, s), BF16),
        jax.ShapeDtypeStruct((b, s, 2 * D_MODEL), BF16),
    )
    out_specs = (
        pl.BlockSpec((1, 2, tm, FOURIER_WIDTH), lambda bi, i: (bi, 0, i, 0)),
        pl.BlockSpec((1, N_HEADS, QK_PAD, tm), lambda bi, i: (bi, 0, 0, i)),
        pl.BlockSpec((1, N_HEADS, tm, QK_PAD), lambda bi, i: (bi, 0, i, 0)),
        pl.BlockSpec((1, N_HEADS, V_PAD, tm), lambda bi, i: (bi, 0, 0, i)),
        pl.BlockSpec((1, tm, 2 * D_MODEL), lambda bi, i: (bi, i, 0)),
    )
    return pl.pallas_call(
        _proj_kernel,
        grid=(b, s // tm),
        in_specs=in_specs,
        out_specs=out_specs,
        out_shape=out_shape,
        compiler_params=_params(("parallel", "parallel"), 56),
        name="proj",
    )(x, *consts, tabs["cosq"], tabs["sinq"], tabs["cosk"], tabs["sink"])


def _dft1_kernel(m1_ref, ab_ref, y_ref):
    n1, tc = ab_ref.shape[2], ab_ref.shape[3]
    z = ab_ref[0].reshape(2 * n1, tc)
    y = _dot(m1_ref[...], z)
    y_ref[0] = y.astype(BF16).reshape(2, n1, tc)


def _dft1_call(ab, m1):
    b, _, n1, cols = ab.shape
    tc = min(DFT1_COLS, cols)
    return pl.pallas_call(
        _dft1_kernel,
        grid=(b, cols // tc),
        in_specs=[_const_spec(m1.shape), pl.BlockSpec((1, 2, n1, tc), lambda bi, i: (bi, 0, 0, i))],
        out_specs=pl.BlockSpec((1, 2, n1, tc), lambda bi, i: (bi, 0, 0, i)),
        out_shape=jax.ShapeDtypeStruct(ab.shape, BF16),
        compiler_params=_params(("parallel", "parallel"), 32),
        name="dft_stage1",
    )(m1, ab)


def _dft2_kernel(t_ref, y_ref, x_ref):
    n2 = y_ref.shape[3]
    for g in range(DFT2_GROUP):
        yk = jnp.concatenate([y_ref[0, 0, g], y_ref[0, 1, g]], axis=0)
        x_ref[0, g] = _dot(t_ref[g], yk).astype(BF16)


def _dft2_call(y, t):
    b, _, n1, n2, fw = y.shape
    g = DFT2_GROUP
    return pl.pallas_call(
        _dft2_kernel,
        grid=(b, n1 // g),
        in_specs=[pl.BlockSpec((g, n2, 2 * n2), lambda bi, i: (i, 0, 0)),
                  pl.BlockSpec((1, 2, g, n2, fw), lambda bi, i: (bi, 0, i, 0, 0))],
        out_specs=pl.BlockSpec((1, g, n2, fw), lambda bi, i: (bi, i, 0, 0)),
        out_shape=jax.ShapeDtypeStruct((b, n1, n2, fw), BF16),
        compiler_params=_params(("parallel", "parallel"), 32),
        name="dft_stage2",
    )(t, y)


def _attn_kernel(qt_ref, k_ref, vt_ref, o_ref, s_ref, mb_ref, m_ref, acc_ref):
    s_len = k_ref.shape[2]
    tk = ATTN_TK
    nblk = s_len // tk
    qt = qt_ref[0, 0]
    m_ref[...] = jnp.full(m_ref.shape, -1e30, F32)
    acc_ref[...] = jnp.zeros(acc_ref.shape, F32)

    def scores(j, slot):
        off = pl.multiple_of(j * tk, tk)
        sc = _dot(k_ref[0, 0, pl.ds(off, tk), :], qt)
        s_ref[slot] = sc
        mb_ref[slot] = jnp.max(sc, axis=0, keepdims=True)

    def accumulate(j, slot):
        off = pl.multiple_of(j * tk, tk)
        m_old = m_ref[...]
        m_new = jnp.maximum(m_old, mb_ref[slot])
        p = jnp.exp2(s_ref[slot] - m_new).astype(BF16)
        vb = vt_ref[0, 0, :, pl.ds(off, tk)]
        acc_ref[...] = acc_ref[...] * jnp.exp2(m_old - m_new) + _dot(vb, p)
        m_ref[...] = m_new

    unroll = ATTN_UNROLL
    scores(0, 0)

    def body(i, carry):
        for u in range(unroll):
            j = unroll * i + u
            scores(j + 1, (u + 1) % 2)
            accumulate(j, u % 2)
        return carry

    lax.fori_loop(0, nblk // unroll - 1, body, 0)
    for j in range(nblk - unroll, nblk):
        if j + 1 < nblk:
            scores(j + 1, (j + 1) % 2)
        accumulate(j, j % 2)
    acc = acc_ref[...]
    o_ref[0] = (acc[:V_DIM] / acc[V_DIM:V_DIM + 1]).astype(BF16)


def _attn_call(qt, k, vt):
    b, h, _, s = qt.shape
    tq = ATTN_TQ
    return pl.pallas_call(
        _attn_kernel,
        grid=(b, h, s // tq),
        in_specs=[pl.BlockSpec((1, 1, QK_PAD, tq), lambda bi, hi, i: (bi, hi, 0, i)),
                  pl.BlockSpec((1, 1, s, QK_PAD), lambda bi, hi, i: (bi, hi, 0, 0)),
                  pl.BlockSpec((1, 1, V_PAD, s), lambda bi, hi, i: (bi, hi, 0, 0))],
        out_specs=pl.BlockSpec((1, V_DIM, tq), lambda bi, hi, i: (bi, hi, i)),
        out_shape=jax.ShapeDtypeStruct((b, ATTN_WIDTH, s), BF16),
        scratch_shapes=[pltpu.VMEM((2, ATTN_TK, tq), F32), pltpu.VMEM((2, 1, tq), F32),
                        pltpu.VMEM((1, tq), F32), pltpu.VMEM((V_PAD, tq), F32)],
        compiler_params=_params(("parallel", "parallel", "arbitrary"), 48),
        name="attention",
    )(qt, k, vt)


def _layer_norm(v, g, b):
    mu = jnp.mean(v, axis=-1, keepdims=True)
    var = jnp.mean(jnp.square(v - mu), axis=-1, keepdims=True)
    return (v - mu) * lax.rsqrt(var + LN_EPS) * g + b


def _final_kernel(alpha, n_f, x_ref, g_ref, *rest):
    f_refs = rest[:n_f]
    (ot_ref, wf_ref, wa_ref, wo_ref, wgate_ref, wup_ref, wdown_ref,
     ln1g_ref, ln1b_ref, ln2g_ref, ln2b_ref, y_ref) = rest[n_f:]
    fa = jnp.concatenate([r[0] for r in f_refs], axis=0)
    branch_a = _dot(fa, wf_ref[...])
    branch_b = _dot_tn(ot_ref[0], wa_ref[...])
    g = g_ref[0]
    merged = (g[:, :D_MODEL].astype(F32) * branch_a + g[:, D_MODEL:].astype(F32) * branch_b).astype(BF16)
    h = _layer_norm(alpha * x_ref[0] + _dot(merged, wo_ref[...]), ln1g_ref[...], ln1b_ref[...])
    hb = h.astype(BF16)
    ck = D_FF // FFN_CHUNKS
    ffn = None
    for c in range(FFN_CHUNKS):
        gate = _dot(hb, wgate_ref[:, c * ck:(c + 1) * ck])
        up = _dot(hb, wup_ref[:, c * ck:(c + 1) * ck])
        part = _dot((jax.nn.silu(gate) * up).astype(BF16), wdown_ref[c * ck:(c + 1) * ck, :])
        ffn = part if ffn is None else ffn + part
    y_ref[0] = _layer_norm(alpha * h + ffn, ln2g_ref[...], ln2b_ref[...])


def _final_call(x, gates, x3, ot, w, alpha):
    b, s, _ = x.shape
    tm = TOK_TILE
    n_f = tm // DFT_N1
    consts = [w["wf"], w["wa"], w["wo"], w["wgate"], w["wup"], w["wdown"],
              w["ln1g"], w["ln1b"], w["ln2g"], w["ln2b"]]
    in_specs = [pl.BlockSpec((1, tm, D_MODEL), lambda bi, i: (bi, i, 0)),
                pl.BlockSpec((1, tm, 2 * D_MODEL), lambda bi, i: (bi, i, 0))]
    in_specs += [pl.BlockSpec((1, DFT_N1, FOURIER_WIDTH), functools.partial(lambda j, bi, i: (bi, 0, n_f * i + j), j))
                 for j in range(n_f)]
    in_specs += [pl.BlockSpec((1, ATTN_WIDTH, tm), lambda bi, i: (bi, 0, i))]
    in_specs += [_const_spec(c.shape) for c in consts]
    return pl.pallas_call(
        functools.partial(_final_kernel, alpha, n_f),
        grid=(b, s // tm),
        in_specs=in_specs,
        out_specs=pl.BlockSpec((1, tm, D_MODEL), lambda bi, i: (bi, i, 0)),
        out_shape=jax.ShapeDtypeStruct((b, s, D_MODEL), F32),
        compiler_params=_params(("parallel", "parallel"), 56),
        name="merge_ffn",
    )(x, gates, *([x3] * n_f), ot, *consts)


def _cos_sin(num, den):
    ang = (2.0 * math.pi / den) * (num % den).astype(F32)
    return jnp.cos(ang), jnp.sin(ang)


def _channel_dft_table():
    j = lax.broadcasted_iota(jnp.int32, (GROUP_DIM, GROUP_DIM), 0)
    k = lax.broadcasted_iota(jnp.int32, (GROUP_DIM, GROUP_DIM), 1)
    c, s = _cos_sin(j * k, GROUP_DIM)
    return jnp.concatenate([c, s], axis=1) * (GROUP_DIM ** -0.5)


def _dft_tables(s_len):
    n1, n2 = DFT_N1, s_len // DFT_N1
    k1 = lax.broadcasted_iota(jnp.int32, (n1, n1), 0)
    i1 = lax.broadcasted_iota(jnp.int32, (n1, n1), 1)
    c, s = _cos_sin(k1 * i1, n1)
    m1 = jnp.block([[c, -s], [-s, -c]]) * (n1 ** -0.5)
    shape = (n1, n2, n2)
    k = lax.broadcasted_iota(jnp.int32, shape, 0) + n1 * lax.broadcasted_iota(jnp.int32, shape, 1)
    i2 = lax.broadcasted_iota(jnp.int32, shape, 2)
    tc, ts = _cos_sin(k * i2, s_len)
    t = jnp.concatenate([tc, ts], axis=2) * (n2 ** -0.5)
    return m1.astype(BF16), t.astype(BF16)


def _rope_tables(s_len):
    half = ROPE_DIM // 2
    inv_freq = 1.0 / (ROPE_THETA ** (jnp.arange(0, ROPE_DIM, 2, dtype=F32) / ROPE_DIM))
    ang = jnp.arange(s_len, dtype=F32)[:, None] * inv_freq[None, :]
    cos2 = jnp.concatenate([jnp.cos(ang)] * 2, axis=1)
    sin2 = jnp.concatenate([jnp.sin(ang)] * 2, axis=1)
    pad = lambda t: jnp.pad(t, ((0, 0), (NOPE_DIM, QK_PAD - QK_DIM)))
    del half
    return {"cosq": cos2.T, "sinq": sin2.T, "cosk": pad(cos2), "sink": pad(sin2)}


def _rotate_half_cols(w):
    half = ROPE_DIM // 2
    return jnp.concatenate([-w[..., half:], w[..., :half]], axis=-1)


def _layer_weights(w_in, w_fourier, g_q, w_uq, g_kv, w_ukv, w_attn, w_o,
                   ln1_g, ln1_b, w_gate, w_up, w_down, ln2_g, ln2_b):
    o0 = FOURIER_WIDTH
    o1 = o0 + Q_RANK
    o2 = o1 + KV_RANK
    o3 = o2 + ROPE_DIM
    lane_pad = lambda t: jnp.pad(t, ((0, 0), (NOPE_DIM, QK_PAD - QK_DIM)))
    wkr = w_in[:, o2:o3]
    q_scale = (QK_DIM ** -0.5) * math.log2(math.e)
    uq = (w_uq * q_scale).reshape(Q_RANK, N_HEADS, QK_DIM)
    w1 = jnp.pad(uq, ((0, 0), (0, 0), (0, QK_PAD - QK_DIM))).reshape(Q_RANK, N_HEADS * QK_PAD).T
    w2 = _rotate_half_cols(uq[:, :, NOPE_DIM:]).reshape(Q_RANK, N_HEADS * ROPE_DIM).T
    ukv = w_ukv.reshape(KV_RANK, N_HEADS, NOPE_DIM + V_DIM)
    wuk = jnp.pad(ukv[:, :, :NOPE_DIM], ((0, 0), (0, 0), (0, QK_PAD - NOPE_DIM))).reshape(KV_RANK, N_HEADS * QK_PAD)
    wuvt = ukv[:, :, NOPE_DIM:].reshape(KV_RANK, ATTN_WIDTH).T
    row = lambda v: v.reshape(1, -1).astype(F32)
    return {
        "wab": _fold_call(w_in[:, :o0], _channel_dft_table()),
        "wq": w_in[:, o0:o1].astype(BF16), "wkv": w_in[:, o1:o2].astype(BF16),
        "wkr": lane_pad(wkr).astype(BF16), "wkrr": lane_pad(_rotate_half_cols(wkr)).astype(BF16),
        "wg": w_in[:, o3:].astype(BF16), "gq": row(g_q), "gkv": row(g_kv),
        "w1": w1.astype(BF16), "w2": w2.astype(BF16), "wuk": wuk.astype(BF16), "wuvt": wuvt.astype(BF16),
        "wf": w_fourier.astype(BF16), "wa": w_attn.astype(BF16), "wo": w_o.astype(BF16),
        "wgate": w_gate.astype(BF16), "wup": w_up.astype(BF16), "wdown": w_down.astype(BF16),
        "ln1g": row(ln1_g), "ln1b": row(ln1_b), "ln2g": row(ln2_g), "ln2b": row(ln2_b),
    }


def _encoder_layer(x, w, alpha):
    b, s, _ = x.shape
    n1, n2 = DFT_N1, s // DFT_N1
    m1, t = _dft_tables(s)
    ab, qt, k, vt, gates = _proj_call(x, w, _rope_tables(s))
    y = _dft1_call(ab.reshape(b, 2, n1, n2 * FOURIER_WIDTH), m1)
    x3 = _dft2_call(y.reshape(b, 2, n1, n2, FOURIER_WIDTH), t)
    ot = _attn_call(qt, k, vt)
    return _final_call(x, gates, x3.reshape(b, n1, n2 * FOURIER_WIDTH), ot, w, alpha)


def kernel(x_prompt, x_sample, w_in, w_fourier, g_q, w_uq, g_kv, w_ukv, w_attn, w_o, ln1_g, ln1_b, w_gate, w_up, w_down, ln2_g, ln2_b):
    stacked = (w_in, w_fourier, g_q, w_uq, g_kv, w_ukv, w_attn, w_o, ln1_g, ln1_b, w_gate, w_up, w_down, ln2_g, ln2_b)
    depth = w_in.shape[0]
    alpha = (2.0 * depth) ** 0.25
    layers = [_layer_weights(*(t[l] for t in stacked)) for l in range(depth)]
    outs = []
    for x in (x_prompt, x_sample):
        for w in layers:
            x = _encoder_layer(x, w, alpha)
        outs.append(x)
    return tuple(outs)
```

```python
import functools
import math

import jax
import jax.numpy as jnp
from jax import lax
from jax.experimental import pallas as pl
from jax.experimental.pallas import tpu as pltpu

F32 = jnp.float32
BF16 = jnp.bfloat16

D_MODEL = 1024
N_GROUPS = 4
GROUP_DIM = 128
FOURIER_WIDTH = N_GROUPS * GROUP_DIM
N_HEADS = 8
NOPE_DIM = 64
ROPE_DIM = 32
QK_DIM = NOPE_DIM + ROPE_DIM
V_DIM = 64
ATTN_WIDTH = N_HEADS * V_DIM
Q_RANK = 384
KV_RANK = 256
ROPE_THETA = 10000.0
D_FF = 2816
LN_EPS = 1e-5
RMS_EPS = 1e-6

VMEM_BYTES_V7X = 64 * 1024 * 1024
LANES = 128

QK_PAD = 128
V_PAD = 80
DFT_N1 = 128
TOK_TILE = 512
DFT1_COLS = 4096
DFT2_GROUP = 8
ATTN_TQ = 512
ATTN_TK = 512
ATTN_UNROLL = 16
FFN_CHUNKS = 2


def _params(semantics, vmem_mb):
    return pltpu.CompilerParams(dimension_semantics=semantics, vmem_limit_bytes=vmem_mb * 1024 * 1024)


def _const_spec(shape):
    zeros = (0,) * len(shape)
    return pl.BlockSpec(shape, lambda *_: zeros, pipeline_mode=pl.Buffered(1))


def _dot(a, b):
    return jnp.dot(a, b, preferred_element_type=F32)


def _dot_nt(a, b):
    return lax.dot_general(a, b, (((1,), (1,)), ((), ())), preferred_element_type=F32)


def _dot_tn(a, b):
    return lax.dot_general(a, b, (((0,), (0,)), ((), ())), preferred_element_type=F32)


def _fold_kernel(wf_ref, cs_ref, out_ref):
    for g in range(N_GROUPS):
        w = wf_ref[:, g * GROUP_DIM:(g + 1) * GROUP_DIM]
        r = jnp.dot(w, cs_ref[...], preferred_element_type=F32, precision=lax.Precision.HIGHEST)
        out_ref[:, g * GROUP_DIM:(g + 1) * GROUP_DIM] = r[:, :GROUP_DIM].astype(BF16)
        out_ref[:, FOURIER_WIDTH + g * GROUP_DIM:FOURIER_WIDTH + (g + 1) * GROUP_DIM] = r[:, GROUP_DIM:].astype(BF16)


def _fold_call(w_f, cs):
    return pl.pallas_call(
        _fold_kernel,
        out_shape=jax.ShapeDtypeStruct((D_MODEL, 2 * FOURIER_WIDTH), BF16),
        name="fold_channel_dft",
    )(w_f, cs)


def _proj_kernel(x_ref, wab_ref, wq_ref, wkv_ref, wkr_ref, wkrr_ref, wg_ref, gq_ref, gkv_ref,
                 w1_ref, w2_ref, wuk_ref, wuvt_ref, cosq_ref, sinq_ref, cosk_ref, sink_ref,
                 ab_ref, qt_ref, k_ref, vt_ref, g_ref):
    xb = x_ref[0].astype(BF16)
    ab = _dot(xb, wab_ref[...])
    ab_ref[0, 0] = ab[:, :FOURIER_WIDTH].astype(BF16)
    ab_ref[0, 1] = ab[:, FOURIER_WIDTH:].astype(BF16)
    g_ref[0] = jax.nn.sigmoid(_dot(xb, wg_ref[...])).astype(BF16)

    cq = _dot(xb, wq_ref[...])
    cqn = (cq * lax.rsqrt(jnp.mean(jnp.square(cq), axis=-1, keepdims=True) + RMS_EPS) * gq_ref[...]).astype(BF16)
    ckv = _dot(xb, wkv_ref[...])
    ckvn = (ckv * lax.rsqrt(jnp.mean(jnp.square(ckv), axis=-1, keepdims=True) + RMS_EPS) * gkv_ref[...]).astype(BF16)

    kpe = _dot(xb, wkr_ref[...]) * cosk_ref[...] + _dot(xb, wkrr_ref[...]) * sink_ref[...]
    kn = _dot(ckvn, wuk_ref[...])
    for h in range(N_HEADS):
        k_ref[0, h] = (kn[:, h * QK_PAD:(h + 1) * QK_PAD] + kpe).astype(BF16)

    qt = _dot_nt(w1_ref[...], cqn)
    qr = _dot_nt(w2_ref[...], cqn)
    cos = cosq_ref[...]
    sin = sinq_ref[...]
    for h in range(N_HEADS):
        r0 = h * QK_PAD
        qt_ref[0, h, 0:NOPE_DIM, :] = qt[r0:r0 + NOPE_DIM].astype(BF16)
        qt_ref[0, h, NOPE_DIM:QK_DIM, :] = (
            qt[r0 + NOPE_DIM:r0 + QK_DIM] * cos + qr[h * ROPE_DIM:(h + 1) * ROPE_DIM] * sin).astype(BF16)
        qt_ref[0, h, QK_DIM:QK_PAD, :] = jnp.zeros((QK_PAD - QK_DIM, qt.shape[1]), BF16)

    vt = _dot_nt(wuvt_ref[...], ckvn)
    for h in range(N_HEADS):
        vt_ref[0, h, 0:V_DIM, :] = vt[h * V_DIM:(h + 1) * V_DIM].astype(BF16)
        vt_ref[0, h, V_DIM:V_PAD, :] = jnp.ones((V_PAD - V_DIM, vt.shape[1]), BF16)


def _proj_call(x, w, tabs):
    b, s, _ = x.shape
    tm = TOK_TILE
    consts = [w["wab"], w["wq"], w["wkv"], w["wkr"], w["wkrr"], w["wg"], w["gq"], w["gkv"],
              w["w1"], w["w2"], w["wuk"], w["wuvt"]]
    in_specs = [pl.BlockSpec((1, tm, D_MODEL), lambda bi, i: (bi, i, 0))]
    in_specs += [_const_spec(c.shape) for c in consts]
    in_specs += [
        pl.BlockSpec((ROPE_DIM, tm), lambda bi, i: (0, i)),
        pl.BlockSpec((ROPE_DIM, tm), lambda bi, i: (0, i)),
        pl.BlockSpec((tm, QK_PAD), lambda bi, i: (i, 0)),
        pl.BlockSpec((tm, QK_PAD), lambda bi, i: (i, 0)),
    ]
    out_shape = (
        jax.ShapeDtypeStruct((b, 2, s, FOURIER_WIDTH), BF16),
        jax.ShapeDtypeStruct((b, N_HEADS, QK_PAD, s), BF16),
        jax.ShapeDtypeStruct((b, N_HEADS, s, QK_PAD), BF16),
        jax.ShapeDtypeStruct((b, N_HEADS, V_PAD, s), BF16),
        jax.ShapeDtypeStruct((b, s, 2 * D_MODEL), BF16),
    )
    out_specs = (
        pl.BlockSpec((1, 2, tm, FOURIER_WIDTH), lambda bi, i: (bi, 0, i, 0)),
        pl.BlockSpec((1, N_HEADS, QK_PAD, tm), lambda bi, i: (bi, 0, 0, i)),
        pl.BlockSpec((1, N_HEADS, tm, QK_PAD), lambda bi, i: (bi, 0, i, 0)),
        pl.BlockSpec((1, N_HEADS, V_PAD, tm), lambda bi, i: (bi, 0, 0, i)),
        pl.BlockSpec((1, tm, 2 * D_MODEL), lambda bi, i: (bi, i, 0)),
    )
    return pl.pallas_call(
        _proj_kernel,
        grid=(b, s // tm),
        in_specs=in_specs,
        out_specs=out_specs,
        out_shape=out_shape,
        compiler_params=_params(("parallel", "parallel"), 56),
        name="proj",
    )(x, *consts, tabs["cosq"], tabs["sinq"], tabs["cosk"], tabs["sink"])


def _dft1_kernel(m1_ref, ab_ref, y_ref):
    n1, tc = ab_ref.shape[2], ab_ref.shape[3]
    z = ab_ref[0].reshape(2 * n1, tc)
    y = _dot(m1_ref[...], z)
    y_ref[0] = y.astype(BF16).reshape(2, n1, tc)


def _dft1_call(ab, m1):
    b, _, n1, cols = ab.shape
    tc = min(DFT1_COLS, cols)
    return pl.pallas_call(
        _dft1_kernel,
        grid=(b, cols // tc),
        in_specs=[_const_spec(m1.shape), pl.BlockSpec((1, 2, n1, tc), lambda bi, i: (bi, 0, 0, i))],
        out_specs=pl.BlockSpec((1, 2, n1, tc), lambda bi, i: (bi, 0, 0, i)),
        out_shape=jax.ShapeDtypeStruct(ab.shape, BF16),
        compiler_params=_params(("parallel", "parallel"), 32),
        name="dft_stage1",
    )(m1, ab)


def _dft2_kernel(t_ref, y_ref, x_ref):
    n2 = y_ref.shape[3]
    for g in range(DFT2_GROUP):
        yk = jnp.concatenate([y_ref[0, 0, g], y_ref[0, 1, g]], axis=0)
        x_ref[0, g] = _dot(t_ref[g], yk).astype(BF16)


def _dft2_call(y, t):
    b, _, n1, n2, fw = y.shape
    g = DFT2_GROUP
    return pl.pallas_call(
        _dft2_kernel,
        grid=(b, n1 // g),
        in_specs=[pl.BlockSpec((g, n2, 2 * n2), lambda bi, i: (i, 0, 0)),
                  pl.BlockSpec((1, 2, g, n2, fw), lambda bi, i: (bi, 0, i, 0, 0))],
        out_specs=pl.BlockSpec((1, g, n2, fw), lambda bi, i: (bi, i, 0, 0)),
        out_shape=jax.ShapeDtypeStruct((b, n1, n2, fw), BF16),
        compiler_params=_params(("parallel", "parallel"), 32),
        name="dft_stage2",
    )(t, y)


def _attn_kernel(qt_ref, k_ref, vt_ref, o_ref, s_ref, mb_ref, m_ref, acc_ref):
    s_len = k_ref.shape[2]
    tq, tk = ATTN_TQ, ATTN_TK
    unroll = min(ATTN_UNROLL, s_len // tk)
    n_qt = s_len // tq
    trips_per_qt = (s_len // tk) // unroll
    m_ref[...] = jnp.full(m_ref.shape, -1e30, F32)
    acc_ref[...] = jnp.zeros(acc_ref.shape, F32)

    def scores(qi, j, slot):
        qoff = pl.multiple_of(qi * tq, tq)
        koff = pl.multiple_of(j * tk, tk)
        sc = _dot(k_ref[0, 0, pl.ds(koff, tk), :], qt_ref[0, 0, :, pl.ds(qoff, tq)])
        s_ref[slot] = sc
        mb_ref[slot] = jnp.max(sc, axis=0, keepdims=True)

    def accumulate(j, slot, first):
        koff = pl.multiple_of(j * tk, tk)
        m_old = m_ref[...]
        if first is not None:
            m_old = jnp.where(first, -1e30, m_old)
        m_new = jnp.maximum(m_old, mb_ref[slot])
        p = jnp.exp2(s_ref[slot] - m_new).astype(BF16)
        vb = vt_ref[0, 0, :, pl.ds(koff, tk)]
        acc_ref[...] = acc_ref[...] * jnp.exp2(m_old - m_new) + _dot(vb, p)
        m_ref[...] = m_new

    scores(0, 0, 0)

    def body(t, carry):
        qi = t // trips_per_qt
        tr = t % trips_per_qt
        for u in range(unroll):
            j = tr * unroll + u
            if u + 1 < unroll:
                scores(qi, j + 1, (u + 1) % 2)
            else:
                t1 = t + 1
                scores(jnp.minimum(t1 // trips_per_qt, n_qt - 1), (t1 % trips_per_qt) * unroll, 0)
            accumulate(j, u % 2, (tr == 0) if u == 0 else None)

        @pl.when(tr == trips_per_qt - 1)
        def _():
            acc = acc_ref[...]
            o_ref[0, :, pl.ds(pl.multiple_of(qi * tq, tq), tq)] = (acc[:V_DIM] / acc[V_DIM:V_DIM + 1]).astype(BF16)

        return carry

    lax.fori_loop(0, n_qt * trips_per_qt, body, 0)


def _attn_call(qt, k, vt):
    b, h, _, s = qt.shape
    return pl.pallas_call(
        _attn_kernel,
        grid=(b, h),
        in_specs=[pl.BlockSpec((1, 1, QK_PAD, s), lambda bi, hi: (bi, hi, 0, 0)),
                  pl.BlockSpec((1, 1, s, QK_PAD), lambda bi, hi: (bi, hi, 0, 0)),
                  pl.BlockSpec((1, 1, V_PAD, s), lambda bi, hi: (bi, hi, 0, 0))],
        out_specs=pl.BlockSpec((1, V_DIM, s), lambda bi, hi: (bi, hi, 0)),
        out_shape=jax.ShapeDtypeStruct((b, ATTN_WIDTH, s), BF16),
        scratch_shapes=[pltpu.VMEM((2, ATTN_TK, ATTN_TQ), F32), pltpu.VMEM((2, 1, ATTN_TQ), F32),
                        pltpu.VMEM((1, ATTN_TQ), F32), pltpu.VMEM((V_PAD, ATTN_TQ), F32)],
        compiler_params=_params(("parallel", "parallel"), 48),
        name="attention",
    )(qt, k, vt)


def _layer_norm(v, g, b):
    mu = jnp.mean(v, axis=-1, keepdims=True)
    var = jnp.mean(jnp.square(v - mu), axis=-1, keepdims=True)
    return (v - mu) * lax.rsqrt(var + LN_EPS) * g + b


def _final_kernel(alpha, n_f, x_ref, g_ref, *rest):
    f_refs = rest[:n_f]
    (ot_ref, wf_ref, wa_ref, wo_ref, wgate_ref, wup_ref, wdown_ref,
     ln1g_ref, ln1b_ref, ln2g_ref, ln2b_ref, y_ref) = rest[n_f:]
    fa = jnp.concatenate([r[0] for r in f_refs], axis=0)
    branch_a = _dot(fa, wf_ref[...])
    branch_b = _dot_tn(ot_ref[0], wa_ref[...])
    g = g_ref[0]
    merged = (g[:, :D_MODEL].astype(F32) * branch_a + g[:, D_MODEL:].astype(F32) * branch_b).astype(BF16)
    h = _layer_norm(alpha * x_ref[0] + _dot(merged, wo_ref[...]), ln1g_ref[...], ln1b_ref[...])
    hb = h.astype(BF16)
    ck = D_FF // FFN_CHUNKS
    ffn = None
    for c in range(FFN_CHUNKS):
        gate = _dot(hb, wgate_ref[:, c * ck:(c + 1) * ck])
        up = _dot(hb, wup_ref[:, c * ck:(c + 1) * ck])
        part = _dot((jax.nn.silu(gate) * up).astype(BF16), wdown_ref[c * ck:(c + 1) * ck, :])
        ffn = part if ffn is None else ffn + part
    y_ref[0] = _layer_norm(alpha * h + ffn, ln2g_ref[...], ln2b_ref[...])


def _final_call(x, gates, x3, ot, w, alpha):
    b, s, _ = x.shape
    tm = TOK_TILE
    n_f = tm // DFT_N1
    consts = [w["wf"], w["wa"], w["wo"], w["wgate"], w["wup"], w["wdown"],
              w["ln1g"], w["ln1b"], w["ln2g"], w["ln2b"]]
    in_specs = [pl.BlockSpec((1, tm, D_MODEL), lambda bi, i: (bi, i, 0)),
                pl.BlockSpec((1, tm, 2 * D_MODEL), lambda bi, i: (bi, i, 0))]
    in_specs += [pl.BlockSpec((1, DFT_N1, FOURIER_WIDTH), functools.partial(lambda j, bi, i: (bi, 0, n_f * i + j), j))
                 for j in range(n_f)]
    in_specs += [pl.BlockSpec((1, ATTN_WIDTH, tm), lambda bi, i: (bi, 0, i))]
    in_specs += [_const_spec(c.shape) for c in consts]
    return pl.pallas_call(
        functools.partial(_final_kernel, alpha, n_f),
        grid=(b, s // tm),
        in_specs=in_specs,
        out_specs=pl.BlockSpec((1, tm, D_MODEL), lambda bi, i: (bi, i, 0)),
        out_shape=jax.ShapeDtypeStruct((b, s, D_MODEL), F32),
        compiler_params=_params(("parallel", "parallel"), 56),
        name="merge_ffn",
    )(x, gates, *([x3] * n_f), ot, *consts)


def _cos_sin(num, den):
    ang = (2.0 * math.pi / den) * (num % den).astype(F32)
    return jnp.cos(ang), jnp.sin(ang)


def _channel_dft_table():
    j = lax.broadcasted_iota(jnp.int32, (GROUP_DIM, GROUP_DIM), 0)
    k = lax.broadcasted_iota(jnp.int32, (GROUP_DIM, GROUP_DIM), 1)
    c, s = _cos_sin(j * k, GROUP_DIM)
    return jnp.concatenate([c, s], axis=1) * (GROUP_DIM ** -0.5)


def _dft_tables(s_len):
    n1, n2 = DFT_N1, s_len // DFT_N1
    k1 = lax.broadcasted_iota(jnp.int32, (n1, n1), 0)
    i1 = lax.broadcasted_iota(jnp.int32, (n1, n1), 1)
    c, s = _cos_sin(k1 * i1, n1)
    m1 = jnp.block([[c, -s], [-s, -c]]) * (n1 ** -0.5)
    shape = (n1, n2, n2)
    k = lax.broadcasted_iota(jnp.int32, shape, 0) + n1 * lax.broadcasted_iota(jnp.int32, shape, 1)
    i2 = lax.broadcasted_iota(jnp.int32, shape, 2)
    tc, ts = _cos_sin(k * i2, s_len)
    t = jnp.concatenate([tc, ts], axis=2) * (n2 ** -0.5)
    return m1.astype(BF16), t.astype(BF16)


def _rope_tables(s_len):
    half = ROPE_DIM // 2
    inv_freq = 1.0 / (ROPE_THETA ** (jnp.arange(0, ROPE_DIM, 2, dtype=F32) / ROPE_DIM))
    ang = jnp.arange(s_len, dtype=F32)[:, None] * inv_freq[None, :]
    cos2 = jnp.concatenate([jnp.cos(ang)] * 2, axis=1)
    sin2 = jnp.concatenate([jnp.sin(ang)] * 2, axis=1)
    pad = lambda t: jnp.pad(t, ((0, 0), (NOPE_DIM, QK_PAD - QK_DIM)))
    del half
    return {"cosq": cos2.T, "sinq": sin2.T, "cosk": pad(cos2), "sink": pad(sin2)}


def _rotate_half_cols(w):
    half = ROPE_DIM // 2
    return jnp.concatenate([-w[..., half:], w[..., :half]], axis=-1)


def _layer_weights(w_in, w_fourier, g_q, w_uq, g_kv, w_ukv, w_attn, w_o,
                   ln1_g, ln1_b, w_gate, w_up, w_down, ln2_g, ln2_b):
    o0 = FOURIER_WIDTH
    o1 = o0 + Q_RANK
    o2 = o1 + KV_RANK
    o3 = o2 + ROPE_DIM
    lane_pad = lambda t: jnp.pad(t, ((0, 0), (NOPE_DIM, QK_PAD - QK_DIM)))
    wkr = w_in[:, o2:o3]
    q_scale = (QK_DIM ** -0.5) * math.log2(math.e)
    uq = (w_uq * q_scale).reshape(Q_RANK, N_HEADS, QK_DIM)
    w1 = jnp.pad(uq, ((0, 0), (0, 0), (0, QK_PAD - QK_DIM))).reshape(Q_RANK, N_HEADS * QK_PAD).T
    w2 = _rotate_half_cols(uq[:, :, NOPE_DIM:]).reshape(Q_RANK, N_HEADS * ROPE_DIM).T
    ukv = w_ukv.reshape(KV_RANK, N_HEADS, NOPE_DIM + V_DIM)
    wuk = jnp.pad(ukv[:, :, :NOPE_DIM], ((0, 0), (0, 0), (0, QK_PAD - NOPE_DIM))).reshape(KV_RANK, N_HEADS * QK_PAD)
    wuvt = ukv[:, :, NOPE_DIM:].reshape(KV_RANK, ATTN_WIDTH).T
    row = lambda v: v.reshape(1, -1).astype(F32)
    return {
        "wab": _fold_call(w_in[:, :o0], _channel_dft_table()),
        "wq": w_in[:, o0:o1].astype(BF16), "wkv": w_in[:, o1:o2].astype(BF16),
        "wkr": lane_pad(wkr).astype(BF16), "wkrr": lane_pad(_rotate_half_cols(wkr)).astype(BF16),
        "wg": w_in[:, o3:].astype(BF16), "gq": row(g_q), "gkv": row(g_kv),
        "w1": w1.astype(BF16), "w2": w2.astype(BF16), "wuk": wuk.astype(BF16), "wuvt": wuvt.astype(BF16),
        "wf": w_fourier.astype(BF16), "wa": w_attn.astype(BF16), "wo": w_o.astype(BF16),
        "wgate": w_gate.astype(BF16), "wup": w_up.astype(BF16), "wdown": w_down.astype(BF16),
        "ln1g": row(ln1_g), "ln1b": row(ln1_b), "ln2g": row(ln2_g), "ln2b": row(ln2_b),
    }


def _encoder_layer(x, w, alpha):
    b, s, _ = x.shape
    n1, n2 = DFT_N1, s // DFT_N1
    m1, t = _dft_tables(s)
    ab, qt, k, vt, gates = _proj_call(x, w, _rope_tables(s))
    y = _dft1_call(ab.reshape(b, 2, n1, n2 * FOURIER_WIDTH), m1)
    x3 = _dft2_call(y.reshape(b, 2, n1, n2, FOURIER_WIDTH), t)
    ot = _attn_call(qt, k, vt)
    return _final_call(x, gates, x3.reshape(b, n1, n2 * FOURIER_WIDTH), ot, w, alpha)


def kernel(x_prompt, x_sample, w_in, w_fourier, g_q, w_uq, g_kv, w_ukv, w_attn, w_o, ln1_g, ln1_b, w_gate, w_up, w_down, ln2_g, ln2_b):
    stacked = (w_in, w_fourier, g_q, w_uq, g_kv, w_ukv, w_attn, w_o, ln1_g, ln1_b, w_gate, w_up, w_down, ln2_g, ln2_b)
    depth = w_in.shape[0]
    alpha = (2.0 * depth) ** 0.25
    layers = [_layer_weights(*(t[l] for t in stacked)) for l in range(depth)]
    outs = []
    for x in (x_prompt, x_sample):
        for w in layers:
            x = _encoder_layer(x, w, alpha)
        outs.append(x)
    return tuple(outs)
```

```python
import functools
import math

import jax
import jax.numpy as jnp
from jax import lax
from jax.experimental import pallas as pl
from jax.experimental.pallas import tpu as pltpu

F32 = jnp.float32
BF16 = jnp.bfloat16

D_MODEL = 1024
N_GROUPS = 4
GROUP_DIM = 128
FOURIER_WIDTH = N_GROUPS * GROUP_DIM
N_HEADS = 8
NOPE_DIM = 64
ROPE_DIM = 32
QK_DIM = NOPE_DIM + ROPE_DIM
V_DIM = 64
ATTN_WIDTH = N_HEADS * V_DIM
Q_RANK = 384
KV_RANK = 256
ROPE_THETA = 10000.0
D_FF = 2816
LN_EPS = 1e-5
RMS_EPS = 1e-6

VMEM_BYTES_V7X = 64 * 1024 * 1024
LANES = 128

QK_PAD = 128
V_PAD = 80
DFT_N1 = 128
TOK_TILE = 512
DFT1_COLS = 4096
DFT2_GROUP = 8
ATTN_TQ = 512
ATTN_TK = 512
ATTN_UNROLL = 16
ATTN_BIAS_ROW = QK_DIM
ATTN_BOUND_SLACK = 1.0 + 2.0 ** -6
ATTN_DENOM_MIN = 2.0 ** -60
FFN_CHUNKS = 2


def _params(semantics, vmem_mb):
    return pltpu.CompilerParams(dimension_semantics=semantics, vmem_limit_bytes=vmem_mb * 1024 * 1024)


def _const_spec(shape):
    zeros = (0,) * len(shape)
    return pl.BlockSpec(shape, lambda *_: zeros, pipeline_mode=pl.Buffered(1))


def _dot(a, b):
    return jnp.dot(a, b, preferred_element_type=F32)


def _dot_nt(a, b):
    return lax.dot_general(a, b, (((1,), (1,)), ((), ())), preferred_element_type=F32)


def _dot_tn(a, b):
    return lax.dot_general(a, b, (((0,), (0,)), ((), ())), preferred_element_type=F32)


def _fold_kernel(wf_ref, cs_ref, out_ref):
    for g in range(N_GROUPS):
        w = wf_ref[:, g * GROUP_DIM:(g + 1) * GROUP_DIM]
        r = jnp.dot(w, cs_ref[...], preferred_element_type=F32, precision=lax.Precision.HIGHEST)
        out_ref[:, g * GROUP_DIM:(g + 1) * GROUP_DIM] = r[:, :GROUP_DIM].astype(BF16)
        out_ref[:, FOURIER_WIDTH + g * GROUP_DIM:FOURIER_WIDTH + (g + 1) * GROUP_DIM] = r[:, GROUP_DIM:].astype(BF16)


def _fold_call(w_f, cs):
    return pl.pallas_call(
        _fold_kernel,
        out_shape=jax.ShapeDtypeStruct((D_MODEL, 2 * FOURIER_WIDTH), BF16),
        name="fold_channel_dft",
    )(w_f, cs)


def _proj_kernel(x_ref, wab_ref, wq_ref, wkv_ref, wkr_ref, wkrr_ref, wg_ref, gq_ref, gkv_ref,
                 w1_ref, w2_ref, wuk_ref, wuvt_ref, cosq_ref, sinq_ref, cosk_ref, sink_ref,
                 ab_ref, qt_ref, k_ref, vt_ref, g_ref):
    xb = x_ref[0].astype(BF16)
    ab = _dot(xb, wab_ref[...])
    ab_ref[0, 0] = ab[:, :FOURIER_WIDTH].astype(BF16)
    ab_ref[0, 1] = ab[:, FOURIER_WIDTH:].astype(BF16)
    g_ref[0] = jax.nn.sigmoid(_dot(xb, wg_ref[...])).astype(BF16)

    cq = _dot(xb, wq_ref[...])
    cqn = (cq * lax.rsqrt(jnp.mean(jnp.square(cq), axis=-1, keepdims=True) + RMS_EPS) * gq_ref[...]).astype(BF16)
    ckv = _dot(xb, wkv_ref[...])
    ckvn = (ckv * lax.rsqrt(jnp.mean(jnp.square(ckv), axis=-1, keepdims=True) + RMS_EPS) * gkv_ref[...]).astype(BF16)

    kpe = _dot(xb, wkr_ref[...]) * cosk_ref[...] + _dot(xb, wkrr_ref[...]) * sink_ref[...]
    lane = lax.broadcasted_iota(jnp.int32, (1, QK_PAD), 1)
    kpe = kpe + jnp.where(lane == ATTN_BIAS_ROW, 1.0, 0.0)
    kn = _dot(ckvn, wuk_ref[...])
    for h in range(N_HEADS):
        k_ref[0, h] = (kn[:, h * QK_PAD:(h + 1) * QK_PAD] + kpe).astype(BF16)

    qt = _dot_nt(w1_ref[...], cqn)
    qr = _dot_nt(w2_ref[...], cqn)
    cos = cosq_ref[...]
    sin = sinq_ref[...]
    for h in range(N_HEADS):
        r0 = h * QK_PAD
        qt_ref[0, h, 0:NOPE_DIM, :] = qt[r0:r0 + NOPE_DIM].astype(BF16)
        qt_ref[0, h, NOPE_DIM:QK_DIM, :] = (
            qt[r0 + NOPE_DIM:r0 + QK_DIM] * cos + qr[h * ROPE_DIM:(h + 1) * ROPE_DIM] * sin).astype(BF16)
        qt_ref[0, h, QK_DIM:QK_PAD, :] = jnp.zeros((QK_PAD - QK_DIM, qt.shape[1]), BF16)

    vt = _dot_nt(wuvt_ref[...], ckvn)
    for h in range(N_HEADS):
        vt_ref[0, h, 0:V_DIM, :] = vt[h * V_DIM:(h + 1) * V_DIM].astype(BF16)
        vt_ref[0, h, V_DIM:V_PAD, :] = jnp.ones((V_PAD - V_DIM, vt.shape[1]), BF16)


def _proj_call(x, w, tabs):
    b, s, _ = x.shape
    tm = TOK_TILE
    consts = [w["wab"], w["wq"], w["wkv"], w["wkr"], w["wkrr"], w["wg"], w["gq"], w["gkv"],
              w["w1"], w["w2"], w["wuk"], w["wuvt"]]
    in_specs = [pl.BlockSpec((1, tm, D_MODEL), lambda bi, i: (bi, i, 0))]
    in_specs += [_const_spec(c.shape) for c in consts]
    in_specs += [
        pl.BlockSpec((ROPE_DIM, tm), lambda bi, i: (0, i)),
        pl.BlockSpec((ROPE_DIM, tm), lambda bi, i: (0, i)),
        pl.BlockSpec((tm, QK_PAD), lambda bi, i: (i, 0)),
        pl.BlockSpec((tm, QK_PAD), lambda bi, i: (i, 0)),
    ]
    out_shape = (
        jax.ShapeDtypeStruct((b, 2, s, FOURIER_WIDTH), BF16),
        jax.ShapeDtypeStruct((b, N_HEADS, QK_PAD, s), BF16),
        jax.ShapeDtypeStruct((b, N_HEADS, s, QK_PAD), BF16),
        jax.ShapeDtypeStruct((b, N_HEADS, V_PAD, s), BF16),
        jax.ShapeDtypeStruct((b, s, 2 * D_MODEL), BF16),
    )
    out_specs = (
        pl.BlockSpec((1, 2, tm, FOURIER_WIDTH), lambda bi, i: (bi, 0, i, 0)),
        pl.BlockSpec((1, N_HEADS, QK_PAD, tm), lambda bi, i: (bi, 0, 0, i)),
        pl.BlockSpec((1, N_HEADS, tm, QK_PAD), lambda bi, i: (bi, 0, i, 0)),
        pl.BlockSpec((1, N_HEADS, V_PAD, tm), lambda bi, i: (bi, 0, 0, i)),
        pl.BlockSpec((1, tm, 2 * D_MODEL), lambda bi, i: (bi, i, 0)),
    )
    return pl.pallas_call(
        _proj_kernel,
        grid=(b, s // tm),
        in_specs=in_specs,
        out_specs=out_specs,
        out_shape=out_shape,
        compiler_params=_params(("parallel", "parallel"), 56),
        name="proj",
    )(x, *consts, tabs["cosq"], tabs["sinq"], tabs["cosk"], tabs["sink"])


def _dft1_kernel(m1_ref, ab_ref, y_ref):
    n1, tc = ab_ref.shape[2], ab_ref.shape[3]
    z = ab_ref[0].reshape(2 * n1, tc)
    y = _dot(m1_ref[...], z)
    y_ref[0] = y.astype(BF16).reshape(2, n1, tc)


def _dft1_call(ab, m1):
    b, _, n1, cols = ab.shape
    tc = min(DFT1_COLS, cols)
    return pl.pallas_call(
        _dft1_kernel,
        grid=(b, cols // tc),
        in_specs=[_const_spec(m1.shape), pl.BlockSpec((1, 2, n1, tc), lambda bi, i: (bi, 0, 0, i))],
        out_specs=pl.BlockSpec((1, 2, n1, tc), lambda bi, i: (bi, 0, 0, i)),
        out_shape=jax.ShapeDtypeStruct(ab.shape, BF16),
        compiler_params=_params(("parallel", "parallel"), 32),
        name="dft_stage1",
    )(m1, ab)


def _dft2_kernel(t_ref, y_ref, x_ref):
    n2 = y_ref.shape[3]
    for g in range(DFT2_GROUP):
        yk = jnp.concatenate([y_ref[0, 0, g], y_ref[0, 1, g]], axis=0)
        x_ref[0, g] = _dot(t_ref[g], yk).astype(BF16)


def _dft2_call(y, t):
    b, _, n1, n2, fw = y.shape
    g = DFT2_GROUP
    return pl.pallas_call(
        _dft2_kernel,
        grid=(b, n1 // g),
        in_specs=[pl.BlockSpec((g, n2, 2 * n2), lambda bi, i: (i, 0, 0)),
                  pl.BlockSpec((1, 2, g, n2, fw), lambda bi, i: (bi, 0, i, 0, 0))],
        out_specs=pl.BlockSpec((1, g, n2, fw), lambda bi, i: (bi, i, 0, 0)),
        out_shape=jax.ShapeDtypeStruct((b, n1, n2, fw), BF16),
        compiler_params=_params(("parallel", "parallel"), 32),
        name="dft_stage2",
    )(t, y)


def _attn_kernel(qt_ref, k_ref, vt_ref, o_ref, qa_ref, m_ref, acc_ref):
    s_len = k_ref.shape[2]
    tq, tk = ATTN_TQ, ATTN_TK
    nblk = s_len // tk
    unroll = min(ATTN_UNROLL, nblk)

    def kv_block(j):
        koff = j * tk if isinstance(j, int) else pl.multiple_of(j * tk, tk)
        return k_ref[0, 0, pl.ds(koff, tk), :], vt_ref[0, 0, :, pl.ds(koff, tk)]

    ones = jnp.ones((16, QK_PAD), BF16)

    def key_norm(i, mx):
        for u in range(unroll):
            kf = kv_block(i * unroll + u)[0].astype(F32)
            mx = jnp.maximum(mx, _dot_nt(ones, (kf * kf).astype(BF16)))
        return mx

    ksq = jnp.max(lax.fori_loop(0, nblk // unroll, key_norm, jnp.zeros((16, tk), F32)))
    row = lax.broadcasted_iota(jnp.int32, (QK_PAD, tq), 0)

    n_qt = s_len // tq
    trips_per_qt = nblk // unroll

    def shifted_q(qi):
        qf = qt_ref[0, 0, :, pl.ds(pl.multiple_of(qi * tq, tq), tq)].astype(F32)
        bound = jnp.sqrt(jnp.sum(qf * qf, axis=0, keepdims=True) * ksq) * ATTN_BOUND_SLACK
        return jnp.where(row == ATTN_BIAS_ROW, -bound, qf).astype(BF16)

    def finish_tile(qi):
        qoff = pl.multiple_of(qi * tq, tq)

        @pl.when(jnp.logical_not(jnp.min(acc_ref[V_DIM:V_DIM + 1, :]) >= ATTN_DENOM_MIN))
        def _():
            qt = qt_ref[0, 0, :, pl.ds(qoff, tq)]
            m_ref[...] = jnp.full(m_ref.shape, -1e30, F32)
            acc_ref[...] = jnp.zeros(acc_ref.shape, F32)

            def exact_block(j, c):
                kb, vb = kv_block(j)
                sc = _dot(kb, qt)
                m_old = m_ref[...]
                m_new = jnp.maximum(m_old, jnp.max(sc, axis=0, keepdims=True))
                acc_ref[...] = acc_ref[...] * jnp.exp2(m_old - m_new) + _dot(vb, jnp.exp2(sc - m_new).astype(BF16))
                m_ref[...] = m_new
                return c

            lax.fori_loop(0, nblk, exact_block, 0)

        acc = acc_ref[...]
        o_ref[0, :, pl.ds(qoff, tq)] = (acc[:V_DIM] / acc[V_DIM:V_DIM + 1]).astype(BF16)

    def q_tile(qi, carry):
        qa_ref[...] = shifted_q(qi)
        acc_ref[...] = jnp.zeros(acc_ref.shape, F32)

        def trip(t, c):
            part = None
            sc = _dot(kv_block(t * unroll)[0], qa_ref[...])
            for u in range(unroll):
                sc_next = _dot(kv_block(t * unroll + u + 1)[0], qa_ref[...]) if u + 1 < unroll else None
                pv = _dot(kv_block(t * unroll + u)[1], jnp.exp2(sc).astype(BF16))
                part = pv if part is None else part + pv
                sc = sc_next
            acc_ref[...] += part
            return c

        lax.fori_loop(0, trips_per_qt, trip, 0)
        finish_tile(qi)
        return carry

    lax.fori_loop(0, n_qt, q_tile, 0)


def _attn_call(qt, k, vt):
    b, h, _, s = qt.shape
    return pl.pallas_call(
        _attn_kernel,
        grid=(b, h),
        in_specs=[pl.BlockSpec((1, 1, QK_PAD, s), lambda bi, hi: (bi, hi, 0, 0)),
                  pl.BlockSpec((1, 1, s, QK_PAD), lambda bi, hi: (bi, hi, 0, 0)),
                  pl.BlockSpec((1, 1, V_PAD, s), lambda bi, hi: (bi, hi, 0, 0))],
        out_specs=pl.BlockSpec((1, V_DIM, s), lambda bi, hi: (bi, hi, 0)),
        out_shape=jax.ShapeDtypeStruct((b, ATTN_WIDTH, s), BF16),
        scratch_shapes=[pltpu.VMEM((QK_PAD, ATTN_TQ), BF16), pltpu.VMEM((1, ATTN_TQ), F32),
                        pltpu.VMEM((V_PAD, ATTN_TQ), F32)],
        compiler_params=_params(("parallel", "parallel"), 48),
        name="attention",
    )(qt, k, vt)


def _layer_norm(v, g, b):
    mu = jnp.mean(v, axis=-1, keepdims=True)
    var = jnp.mean(jnp.square(v - mu), axis=-1, keepdims=True)
    return (v - mu) * lax.rsqrt(var + LN_EPS) * g + b


def _final_kernel(alpha, n_f, x_ref, g_ref, *rest):
    f_refs = rest[:n_f]
    (ot_ref, wf_ref, wa_ref, wo_ref, wgate_ref, wup_ref, wdown_ref,
     ln1g_ref, ln1b_ref, ln2g_ref, ln2b_ref, y_ref) = rest[n_f:]
    fa = jnp.concatenate([r[0] for r in f_refs], axis=0)
    branch_a = _dot(fa, wf_ref[...])
    branch_b = _dot_tn(ot_ref[0], wa_ref[...])
    g = g_ref[0]
    merged = (g[:, :D_MODEL].astype(F32) * branch_a + g[:, D_MODEL:].astype(F32) * branch_b).astype(BF16)
    h = _layer_norm(alpha * x_ref[0] + _dot(merged, wo_ref[...]), ln1g_ref[...], ln1b_ref[...])
    hb = h.astype(BF16)
    ck = D_FF // FFN_CHUNKS
    ffn = None
    for c in range(FFN_CHUNKS):
        gate = _dot(hb, wgate_ref[:, c * ck:(c + 1) * ck])
        up = _dot(hb, wup_ref[:, c * ck:(c + 1) * ck])
        part = _dot((jax.nn.silu(gate) * up).astype(BF16), wdown_ref[c * ck:(c + 1) * ck, :])
        ffn = part if ffn is None else ffn + part
    y_ref[0] = _layer_norm(alpha * h + ffn, ln2g_ref[...], ln2b_ref[...])


def _final_call(x, gates, x3, ot, w, alpha):
    b, s, _ = x.shape
    tm = TOK_TILE
    n_f = tm // DFT_N1
    consts = [w["wf"], w["wa"], w["wo"], w["wgate"], w["wup"], w["wdown"],
              w["ln1g"], w["ln1b"], w["ln2g"], w["ln2b"]]
    in_specs = [pl.BlockSpec((1, tm, D_MODEL), lambda bi, i: (bi, i, 0)),
                pl.BlockSpec((1, tm, 2 * D_MODEL), lambda bi, i: (bi, i, 0))]
    in_specs += [pl.BlockSpec((1, DFT_N1, FOURIER_WIDTH), functools.partial(lambda j, bi, i: (bi, 0, n_f * i + j), j))
                 for j in range(n_f)]
    in_specs += [pl.BlockSpec((1, ATTN_WIDTH, tm), lambda bi, i: (bi, 0, i))]
    in_specs += [_const_spec(c.shape) for c in consts]
    return pl.pallas_call(
        functools.partial(_final_kernel, alpha, n_f),
        grid=(b, s // tm),
        in_specs=in_specs,
        out_specs=pl.BlockSpec((1, tm, D_MODEL), lambda bi, i: (bi, i, 0)),
        out_shape=jax.ShapeDtypeStruct((b, s, D_MODEL), F32),
        compiler_params=_params(("parallel", "parallel"), 56),
        name="merge_ffn",
    )(x, gates, *([x3] * n_f), ot, *consts)


def _cos_sin(num, den):
    ang = (2.0 * math.pi / den) * (num % den).astype(F32)
    return jnp.cos(ang), jnp.sin(ang)


def _channel_dft_table():
    j = lax.broadcasted_iota(jnp.int32, (GROUP_DIM, GROUP_DIM), 0)
    k = lax.broadcasted_iota(jnp.int32, (GROUP_DIM, GROUP_DIM), 1)
    c, s = _cos_sin(j * k, GROUP_DIM)
    return jnp.concatenate([c, s], axis=1) * (GROUP_DIM ** -0.5)


def _dft_tables(s_len):
    n1, n2 = DFT_N1, s_len // DFT_N1
    k1 = lax.broadcasted_iota(jnp.int32, (n1, n1), 0)
    i1 = lax.broadcasted_iota(jnp.int32, (n1, n1), 1)
    c, s = _cos_sin(k1 * i1, n1)
    m1 = jnp.block([[c, -s], [-s, -c]]) * (n1 ** -0.5)
    shape = (n1, n2, n2)
    k = lax.broadcasted_iota(jnp.int32, shape, 0) + n1 * lax.broadcasted_iota(jnp.int32, shape, 1)
    i2 = lax.broadcasted_iota(jnp.int32, shape, 2)
    tc, ts = _cos_sin(k * i2, s_len)
    t = jnp.concatenate([tc, ts], axis=2) * (n2 ** -0.5)
    return m1.astype(BF16), t.astype(BF16)


def _rope_tables(s_len):
    half = ROPE_DIM // 2
    inv_freq = 1.0 / (ROPE_THETA ** (jnp.arange(0, ROPE_DIM, 2, dtype=F32) / ROPE_DIM))
    ang = jnp.arange(s_len, dtype=F32)[:, None] * inv_freq[None, :]
    cos2 = jnp.concatenate([jnp.cos(ang)] * 2, axis=1)
    sin2 = jnp.concatenate([jnp.sin(ang)] * 2, axis=1)
    pad = lambda t: jnp.pad(t, ((0, 0), (NOPE_DIM, QK_PAD - QK_DIM)))
    del half
    return {"cosq": cos2.T, "sinq": sin2.T, "cosk": pad(cos2), "sink": pad(sin2)}


def _rotate_half_cols(w):
    half = ROPE_DIM // 2
    return jnp.concatenate([-w[..., half:], w[..., :half]], axis=-1)


def _layer_weights(w_in, w_fourier, g_q, w_uq, g_kv, w_ukv, w_attn, w_o,
                   ln1_g, ln1_b, w_gate, w_up, w_down, ln2_g, ln2_b):
    o0 = FOURIER_WIDTH
    o1 = o0 + Q_RANK
    o2 = o1 + KV_RANK
    o3 = o2 + ROPE_DIM
    lane_pad = lambda t: jnp.pad(t, ((0, 0), (NOPE_DIM, QK_PAD - QK_DIM)))
    wkr = w_in[:, o2:o3]
    q_scale = (QK_DIM ** -0.5) * math.log2(math.e)
    uq = (w_uq * q_scale).reshape(Q_RANK, N_HEADS, QK_DIM)
    w1 = jnp.pad(uq, ((0, 0), (0, 0), (0, QK_PAD - QK_DIM))).reshape(Q_RANK, N_HEADS * QK_PAD).T
    w2 = _rotate_half_cols(uq[:, :, NOPE_DIM:]).reshape(Q_RANK, N_HEADS * ROPE_DIM).T
    ukv = w_ukv.reshape(KV_RANK, N_HEADS, NOPE_DIM + V_DIM)
    wuk = jnp.pad(ukv[:, :, :NOPE_DIM], ((0, 0), (0, 0), (0, QK_PAD - NOPE_DIM))).reshape(KV_RANK, N_HEADS * QK_PAD)
    wuvt = ukv[:, :, NOPE_DIM:].reshape(KV_RANK, ATTN_WIDTH).T
    row = lambda v: v.reshape(1, -1).astype(F32)
    return {
        "wab": _fold_call(w_in[:, :o0], _channel_dft_table()),
        "wq": w_in[:, o0:o1].astype(BF16), "wkv": w_in[:, o1:o2].astype(BF16),
        "wkr": lane_pad(wkr).astype(BF16), "wkrr": lane_pad(_rotate_half_cols(wkr)).astype(BF16),
        "wg": w_in[:, o3:].astype(BF16), "gq": row(g_q), "gkv": row(g_kv),
        "w1": w1.astype(BF16), "w2": w2.astype(BF16), "wuk": wuk.astype(BF16), "wuvt": wuvt.astype(BF16),
        "wf": w_fourier.astype(BF16), "wa": w_attn.astype(BF16), "wo": w_o.astype(BF16),
        "wgate": w_gate.astype(BF16), "wup": w_up.astype(BF16), "wdown": w_down.astype(BF16),
        "ln1g": row(ln1_g), "ln1b": row(ln1_b), "ln2g": row(ln2_g), "ln2b": row(ln2_b),
    }


def _encoder_layer(x, w, alpha):
    b, s, _ = x.shape
    n1, n2 = DFT_N1, s // DFT_N1
    m1, t = _dft_tables(s)
    ab, qt, k, vt, gates = _proj_call(x, w, _rope_tables(s))
    y = _dft1_call(ab.reshape(b, 2, n1, n2 * FOURIER_WIDTH), m1)
    x3 = _dft2_call(y.reshape(b, 2, n1, n2, FOURIER_WIDTH), t)
    ot = _attn_call(qt, k, vt)
    return _final_call(x, gates, x3.reshape(b, n1, n2 * FOURIER_WIDTH), ot, w, alpha)


def kernel(x_prompt, x_sample, w_in, w_fourier, g_q, w_uq, g_kv, w_ukv, w_attn, w_o, ln1_g, ln1_b, w_gate, w_up, w_down, ln2_g, ln2_b):
    stacked = (w_in, w_fourier, g_q, w_uq, g_kv, w_ukv, w_attn, w_o, ln1_g, ln1_b, w_gate, w_up, w_down, ln2_g, ln2_b)
    depth = w_in.shape[0]
    alpha = (2.0 * depth) ** 0.25
    layers = [_layer_weights(*(t[l] for t in stacked)) for l in range(depth)]
    outs = []
    for x in (x_prompt, x_sample):
        for w in layers:
            x = _encoder_layer(x, w, alpha)
        outs.append(x)
    return tuple(outs)
```

```python
import functools
import math

import jax
import jax.numpy as jnp
import numpy as np
from jax import lax
from jax.experimental import pallas as pl
from jax.experimental.pallas import tpu as pltpu

F32 = jnp.float32
BF16 = jnp.bfloat16

D_MODEL = 1024
N_GROUPS = 4
GROUP_DIM = 128
FOURIER_WIDTH = N_GROUPS * GROUP_DIM
N_HEADS = 8
NOPE_DIM = 64
ROPE_DIM = 32
QK_DIM = NOPE_DIM + ROPE_DIM
V_DIM = 64
ATTN_WIDTH = N_HEADS * V_DIM
Q_RANK = 384
KV_RANK = 256
ROPE_THETA = 10000.0
D_FF = 2816
LN_EPS = 1e-5
RMS_EPS = 1e-6

VMEM_BYTES_V7X = 64 * 1024 * 1024
LANES = 128

QK_PAD = 128
V_PAD = 80
DFT_N1 = 128
TOK_TILE = 512
DFT1_ROWS = 16
DFT2_GROUP = 16
ATTN_TQ = 512
ATTN_TQ_SHORT = 1024
ATTN_TK = 512
ATTN_UNROLL = 16
ATTN_BIAS_ROW = QK_DIM
ATTN_BOUND_SLACK = 1.0 + 2.0 ** -6
ATTN_DENOM_MIN = 2.0 ** -60
FFN_CHUNKS = 2


def _params(semantics, vmem_mb):
    return pltpu.CompilerParams(dimension_semantics=semantics, vmem_limit_bytes=vmem_mb * 1024 * 1024)


def _const_spec(shape):
    zeros = (0,) * len(shape)
    return pl.BlockSpec(shape, lambda *_: zeros, pipeline_mode=pl.Buffered(1))


def _dot(a, b):
    return jnp.dot(a, b, preferred_element_type=F32)


def _dot_nt(a, b):
    return lax.dot_general(a, b, (((1,), (1,)), ((), ())), preferred_element_type=F32)


def _dot_tn(a, b):
    return lax.dot_general(a, b, (((0,), (0,)), ((), ())), preferred_element_type=F32)


def _fold_kernel(wf_ref, cs_ref, out_ref):
    for g in range(N_GROUPS):
        w = wf_ref[:, g * GROUP_DIM:(g + 1) * GROUP_DIM]
        r = jnp.dot(w, cs_ref[...], preferred_element_type=F32, precision=lax.Precision.HIGHEST)
        out_ref[:, g * GROUP_DIM:(g + 1) * GROUP_DIM] = r[:, :GROUP_DIM].astype(BF16)
        out_ref[:, FOURIER_WIDTH + g * GROUP_DIM:FOURIER_WIDTH + (g + 1) * GROUP_DIM] = r[:, GROUP_DIM:].astype(BF16)


def _fold_call(w_f, cs):
    return pl.pallas_call(
        _fold_kernel,
        out_shape=jax.ShapeDtypeStruct((D_MODEL, 2 * FOURIER_WIDTH), BF16),
        name="fold_channel_dft",
    )(w_f, cs)


def _proj_kernel(x_ref, wab_ref, wlat_ref, wg_ref, gq_ref, gkv_ref,
                 w1_ref, w2_ref, wuk_ref, wuvt_ref, cosq_ref, sinq_ref, cosk_ref, sink_ref,
                 ab_ref, qt_ref, k_ref, vt_ref, g_ref):
    xb = x_ref[0].astype(BF16)
    ab = _dot(xb, wab_ref[...])
    ab_ref[0, 0] = ab[:, :FOURIER_WIDTH].astype(BF16)
    ab_ref[0, 1] = ab[:, FOURIER_WIDTH:].astype(BF16)
    g_ref[0] = jax.nn.sigmoid(_dot(xb, wg_ref[...])).astype(BF16)

    o1 = Q_RANK
    o2 = o1 + KV_RANK
    o3 = o2 + QK_PAD
    lat = _dot(xb, wlat_ref[...])
    cq = lat[:, :o1]
    cqn = (cq * lax.rsqrt(jnp.mean(jnp.square(cq), axis=-1, keepdims=True) + RMS_EPS) * gq_ref[...]).astype(BF16)
    ckv = lat[:, o1:o2]
    ckvn = (ckv * lax.rsqrt(jnp.mean(jnp.square(ckv), axis=-1, keepdims=True) + RMS_EPS) * gkv_ref[...]).astype(BF16)

    kpe = lat[:, o2:o3] * cosk_ref[...] + lat[:, o3:] * sink_ref[...]
    lane = lax.broadcasted_iota(jnp.int32, (1, QK_PAD), 1)
    kpe = kpe + jnp.where(lane == ATTN_BIAS_ROW, 1.0, 0.0)
    kn = _dot(ckvn, wuk_ref[...])
    for h in range(N_HEADS):
        k_ref[0, h] = (kn[:, h * QK_PAD:(h + 1) * QK_PAD] + kpe).astype(BF16)

    qt = _dot_nt(w1_ref[...], cqn)
    qr = _dot_nt(w2_ref[...], cqn)
    cos = cosq_ref[...]
    sin = sinq_ref[...]
    for h in range(N_HEADS):
        r0 = h * QK_PAD
        qt_ref[0, h, 0:NOPE_DIM, :] = qt[r0:r0 + NOPE_DIM].astype(BF16)
        qt_ref[0, h, NOPE_DIM:QK_DIM, :] = (
            qt[r0 + NOPE_DIM:r0 + QK_DIM] * cos + qr[h * ROPE_DIM:(h + 1) * ROPE_DIM] * sin).astype(BF16)
        qt_ref[0, h, QK_DIM:QK_PAD, :] = jnp.zeros((QK_PAD - QK_DIM, qt.shape[1]), BF16)

    vt = _dot_nt(wuvt_ref[...], ckvn)
    for h in range(N_HEADS):
        vt_ref[0, h, 0:V_DIM, :] = vt[h * V_DIM:(h + 1) * V_DIM].astype(BF16)
        vt_ref[0, h, V_DIM:V_PAD, :] = jnp.ones((V_PAD - V_DIM, vt.shape[1]), BF16)


def _proj_call(x, w, tabs):
    b, s, _ = x.shape
    tm = TOK_TILE
    consts = [w["wab"], w["wlat"], w["wg"], w["gq"], w["gkv"], w["w1"], w["w2"], w["wuk"], w["wuvt"]]
    in_specs = [pl.BlockSpec((1, tm, D_MODEL), lambda bi, i: (bi, i, 0))]
    in_specs += [_const_spec(c.shape) for c in consts]
    in_specs += [
        pl.BlockSpec((ROPE_DIM, tm), lambda bi, i: (0, i)),
        pl.BlockSpec((ROPE_DIM, tm), lambda bi, i: (0, i)),
        pl.BlockSpec((tm, QK_PAD), lambda bi, i: (i, 0)),
        pl.BlockSpec((tm, QK_PAD), lambda bi, i: (i, 0)),
    ]
    out_shape = (
        jax.ShapeDtypeStruct((b, 2, s, FOURIER_WIDTH), BF16),
        jax.ShapeDtypeStruct((b, N_HEADS, QK_PAD, s), BF16),
        jax.ShapeDtypeStruct((b, N_HEADS, s, QK_PAD), BF16),
        jax.ShapeDtypeStruct((b, N_HEADS, V_PAD, s), BF16),
        jax.ShapeDtypeStruct((b, s, 2 * D_MODEL), BF16),
    )
    out_specs = (
        pl.BlockSpec((1, 2, tm, FOURIER_WIDTH), lambda bi, i: (bi, 0, i, 0)),
        pl.BlockSpec((1, N_HEADS, QK_PAD, tm), lambda bi, i: (bi, 0, 0, i)),
        pl.BlockSpec((1, N_HEADS, tm, QK_PAD), lambda bi, i: (bi, 0, i, 0)),
        pl.BlockSpec((1, N_HEADS, V_PAD, tm), lambda bi, i: (bi, 0, 0, i)),
        pl.BlockSpec((1, tm, 2 * D_MODEL), lambda bi, i: (bi, i, 0)),
    )
    return pl.pallas_call(
        _proj_kernel,
        grid=(b, s // tm),
        in_specs=in_specs,
        out_specs=out_specs,
        out_shape=out_shape,
        compiler_params=_params(("parallel", "parallel"), 56),
        name="proj",
    )(x, *consts, tabs["cosq"], tabs["sinq"], tabs["cosk"], tabs["sink"])


def _dft1_kernel(m1_ref, ab_ref, y_ref):
    _, _, n1, r, fw = ab_ref.shape
    zt = pltpu.einshape("mrc->rmc", ab_ref[0].reshape(2 * n1, r, fw))
    ys = [_dot(m1_ref[...], zt[i]).astype(BF16) for i in range(r)]
    y_ref[0] = pltpu.einshape("rmc->mrc", jnp.stack(ys, axis=0)).reshape(2, n1, r, fw)


def _dft1_call(ab, m1):
    b, _, n1, n2, fw = ab.shape
    r = DFT1_ROWS
    return pl.pallas_call(
        _dft1_kernel,
        grid=(b, n2 // r),
        in_specs=[_const_spec(m1.shape), pl.BlockSpec((1, 2, n1, r, fw), lambda bi, i: (bi, 0, 0, i, 0))],
        out_specs=pl.BlockSpec((1, 2, n1, r, fw), lambda bi, i: (bi, 0, 0, i, 0)),
        out_shape=jax.ShapeDtypeStruct(ab.shape, BF16),
        compiler_params=_params(("parallel", "parallel"), 32),
        name="dft_stage1",
    )(m1, ab)


def _dft2_kernel(t_ref, y_ref, x_ref):
    xs = []
    for g in range(DFT2_GROUP):
        yk = jnp.concatenate([y_ref[0, 0, g], y_ref[0, 1, g]], axis=0)
        xs.append(_dot(t_ref[g], yk).astype(BF16))
    x_ref[0] = pltpu.einshape("gkc->kgc", jnp.stack(xs, axis=0))


def _dft2_call(y, t):
    b, _, n1, n2, fw = y.shape
    g = DFT2_GROUP
    return pl.pallas_call(
        _dft2_kernel,
        grid=(b, n1 // g),
        in_specs=[pl.BlockSpec((g, n2, 2 * n2), lambda bi, i: (i, 0, 0)),
                  pl.BlockSpec((1, 2, g, n2, fw), lambda bi, i: (bi, 0, i, 0, 0))],
        out_specs=pl.BlockSpec((1, n2, g, fw), lambda bi, i: (bi, 0, i, 0)),
        out_shape=jax.ShapeDtypeStruct((b, n2, n1, fw), BF16),
        compiler_params=_params(("parallel", "parallel"), 32),
        name="dft_stage2",
    )(t, y)


def _attn_kernel(qt_ref, k_ref, vt_ref, o_ref, qa_ref, m_ref, acc_ref):
    s_len = k_ref.shape[2]
    tq, tk = qa_ref.shape[1], ATTN_TK
    nblk = s_len // tk
    unroll = min(ATTN_UNROLL, nblk)

    def kv_block(j):
        koff = j * tk if isinstance(j, int) else pl.multiple_of(j * tk, tk)
        return k_ref[0, 0, pl.ds(koff, tk), :], vt_ref[0, 0, :, pl.ds(koff, tk)]

    ones = jnp.ones((16, QK_PAD), BF16)

    def key_norm(i, mx):
        for u in range(unroll):
            kf = kv_block(i * unroll + u)[0].astype(F32)
            mx = jnp.maximum(mx, _dot_nt(ones, (kf * kf).astype(BF16)))
        return mx

    ksq = jnp.max(lax.fori_loop(0, nblk // unroll, key_norm, jnp.zeros((16, tk), F32)))
    row = lax.broadcasted_iota(jnp.int32, (QK_PAD, tq), 0)

    n_qt = s_len // tq
    trips_per_qt = nblk // unroll

    def shifted_q(qi):
        qf = qt_ref[0, 0, :, pl.ds(pl.multiple_of(qi * tq, tq), tq)].astype(F32)
        bound = jnp.sqrt(jnp.sum(qf * qf, axis=0, keepdims=True) * ksq) * ATTN_BOUND_SLACK
        return jnp.where(row == ATTN_BIAS_ROW, -bound, qf).astype(BF16)

    def finish_tile(qi):
        qoff = pl.multiple_of(qi * tq, tq)

        @pl.when(jnp.logical_not(jnp.min(acc_ref[V_DIM:V_DIM + 1, :]) >= ATTN_DENOM_MIN))
        def _():
            qt = qt_ref[0, 0, :, pl.ds(qoff, tq)]
            m_ref[...] = jnp.full(m_ref.shape, -1e30, F32)
            acc_ref[...] = jnp.zeros(acc_ref.shape, F32)

            def exact_block(j, c):
                kb, vb = kv_block(j)
                sc = _dot(kb, qt)
                m_old = m_ref[...]
                m_new = jnp.maximum(m_old, jnp.max(sc, axis=0, keepdims=True))
                acc_ref[...] = acc_ref[...] * jnp.exp2(m_old - m_new) + _dot(vb, jnp.exp2(sc - m_new).astype(BF16))
                m_ref[...] = m_new
                return c

            lax.fori_loop(0, nblk, exact_block, 0)

        acc = acc_ref[...]
        o_ref[0, :, pl.ds(qoff, tq)] = (acc[:V_DIM] / acc[V_DIM:V_DIM + 1]).astype(BF16)

    def q_tile(qi, carry):
        qa_ref[...] = shifted_q(qi)
        acc_ref[...] = jnp.zeros(acc_ref.shape, F32)

        def trip(t, c):
            part = None
            sc = _dot(kv_block(t * unroll)[0], qa_ref[...])
            for u in range(unroll):
                sc_next = _dot(kv_block(t * unroll + u + 1)[0], qa_ref[...]) if u + 1 < unroll else None
                pv = _dot(kv_block(t * unroll + u)[1], jnp.exp2(sc).astype(BF16))
                part = pv if part is None else part + pv
                sc = sc_next
            acc_ref[...] += part
            return c

        lax.fori_loop(0, trips_per_qt, trip, 0)
        finish_tile(qi)
        return carry

    lax.fori_loop(0, n_qt, q_tile, 0)


def _attn_call(qt, k, vt):
    b, h, _, s = qt.shape
    tq = ATTN_TQ_SHORT if s // ATTN_TK <= ATTN_UNROLL else ATTN_TQ
    return pl.pallas_call(
        _attn_kernel,
        grid=(b, h),
        in_specs=[pl.BlockSpec((1, 1, QK_PAD, s), lambda bi, hi: (bi, hi, 0, 0)),
                  pl.BlockSpec((1, 1, s, QK_PAD), lambda bi, hi: (bi, hi, 0, 0)),
                  pl.BlockSpec((1, 1, V_PAD, s), lambda bi, hi: (bi, hi, 0, 0))],
        out_specs=pl.BlockSpec((1, V_DIM, s), lambda bi, hi: (bi, hi, 0)),
        out_shape=jax.ShapeDtypeStruct((b, ATTN_WIDTH, s), BF16),
        scratch_shapes=[pltpu.VMEM((QK_PAD, tq), BF16), pltpu.VMEM((1, tq), F32), pltpu.VMEM((V_PAD, tq), F32)],
        compiler_params=_params(("parallel", "parallel"), 48),
        name="attention",
    )(qt, k, vt)


def _layer_norm(v, g, b):
    mu = jnp.mean(v, axis=-1, keepdims=True)
    var = jnp.mean(jnp.square(v - mu), axis=-1, keepdims=True)
    return (v - mu) * lax.rsqrt(var + LN_EPS) * g + b


def _final_kernel(alpha, x_ref, g_ref, f_ref, ot_ref, wf_ref, wa_ref, wo_ref, wgate_ref, wup_ref, wdown_ref,
                  ln1g_ref, ln1b_ref, ln2g_ref, ln2b_ref, y_ref):
    branch_a = _dot(f_ref[0], wf_ref[...])
    branch_b = _dot_tn(ot_ref[0], wa_ref[...])
    g = g_ref[0]
    merged = (g[:, :D_MODEL].astype(F32) * branch_a + g[:, D_MODEL:].astype(F32) * branch_b).astype(BF16)
    h = _layer_norm(alpha * x_ref[0] + _dot(merged, wo_ref[...]), ln1g_ref[...], ln1b_ref[...])
    hb = h.astype(BF16)
    ck = D_FF // FFN_CHUNKS
    ffn = None
    for c in range(FFN_CHUNKS):
        gate = _dot(hb, wgate_ref[:, c * ck:(c + 1) * ck])
        up = _dot(hb, wup_ref[:, c * ck:(c + 1) * ck])
        part = _dot((jax.nn.silu(gate) * up).astype(BF16), wdown_ref[c * ck:(c + 1) * ck, :])
        ffn = part if ffn is None else ffn + part
    y_ref[0] = _layer_norm(alpha * h + ffn, ln2g_ref[...], ln2b_ref[...])


def _final_call(x, gates, x3, ot, w, alpha):
    b, s, _ = x.shape
    tm = TOK_TILE
    consts = [w["wf"], w["wa"], w["wo"], w["wgate"], w["wup"], w["wdown"],
              w["ln1g"], w["ln1b"], w["ln2g"], w["ln2b"]]
    in_specs = [pl.BlockSpec((1, tm, D_MODEL), lambda bi, i: (bi, i, 0)),
                pl.BlockSpec((1, tm, 2 * D_MODEL), lambda bi, i: (bi, i, 0)),
                pl.BlockSpec((1, tm, FOURIER_WIDTH), lambda bi, i: (bi, i, 0)),
                pl.BlockSpec((1, ATTN_WIDTH, tm), lambda bi, i: (bi, 0, i))]
    in_specs += [_const_spec(c.shape) for c in consts]
    return pl.pallas_call(
        functools.partial(_final_kernel, alpha),
        grid=(b, s // tm),
        in_specs=in_specs,
        out_specs=pl.BlockSpec((1, tm, D_MODEL), lambda bi, i: (bi, i, 0)),
        out_shape=jax.ShapeDtypeStruct((b, s, D_MODEL), F32),
        compiler_params=_params(("parallel", "parallel"), 56),
        name="merge_ffn",
    )(x, gates, x3, ot, *consts)


def _cos_sin(num, den):
    ang = (2.0 * np.pi / den) * (num % den).astype(np.float64)
    return np.cos(ang), np.sin(ang)


def _channel_dft_table():
    j = np.arange(GROUP_DIM)
    c, s = _cos_sin(j[:, None] * j[None, :], GROUP_DIM)
    return (np.concatenate([c, s], axis=1) * GROUP_DIM ** -0.5).astype(np.float32)


def _dft_tables(s_len):
    n1, n2 = DFT_N1, s_len // DFT_N1
    i1 = np.arange(n1)
    c, s = _cos_sin(i1[:, None] * i1[None, :], n1)
    m1 = np.block([[c, -s], [-s, -c]]) * n1 ** -0.5
    i2 = np.arange(n2)
    k = i1[:, None, None] + n1 * i2[None, :, None]
    tc, ts = _cos_sin(k * i2[None, None, :], s_len)
    t = np.concatenate([tc, ts], axis=2) * n2 ** -0.5
    return m1.astype(BF16), t.astype(BF16)


def _rope_tables(s_len):
    f32 = np.float32
    inv_freq = 1.0 / ROPE_THETA ** (np.arange(0, ROPE_DIM, 2, dtype=np.float64) / ROPE_DIM)
    ang = np.arange(s_len, dtype=np.float64)[:, None] * inv_freq[None, :]
    cos2 = np.concatenate([np.cos(ang)] * 2, axis=1).astype(f32)
    sin2 = np.concatenate([np.sin(ang)] * 2, axis=1).astype(f32)
    pad = lambda t: np.pad(t, ((0, 0), (NOPE_DIM, QK_PAD - QK_DIM)))
    return {"cosq": np.ascontiguousarray(cos2.T), "sinq": np.ascontiguousarray(sin2.T),
            "cosk": pad(cos2), "sink": pad(sin2)}


def _rotate_half_cols(w):
    half = ROPE_DIM // 2
    return jnp.concatenate([-w[..., half:], w[..., :half]], axis=-1)


def _layer_weights(w_in, w_fourier, g_q, w_uq, g_kv, w_ukv, w_attn, w_o,
                   ln1_g, ln1_b, w_gate, w_up, w_down, ln2_g, ln2_b):
    o0 = FOURIER_WIDTH
    o1 = o0 + Q_RANK
    o2 = o1 + KV_RANK
    o3 = o2 + ROPE_DIM
    lane_pad = lambda t: jnp.pad(t, ((0, 0), (NOPE_DIM, QK_PAD - QK_DIM)))
    wkr = w_in[:, o2:o3]
    q_scale = (QK_DIM ** -0.5) * math.log2(math.e)
    uq = (w_uq * q_scale).reshape(Q_RANK, N_HEADS, QK_DIM)
    w1 = jnp.pad(uq, ((0, 0), (0, 0), (0, QK_PAD - QK_DIM))).reshape(Q_RANK, N_HEADS * QK_PAD).T
    w2 = _rotate_half_cols(uq[:, :, NOPE_DIM:]).reshape(Q_RANK, N_HEADS * ROPE_DIM).T
    ukv = w_ukv.reshape(KV_RANK, N_HEADS, NOPE_DIM + V_DIM)
    wuk = jnp.pad(ukv[:, :, :NOPE_DIM], ((0, 0), (0, 0), (0, QK_PAD - NOPE_DIM))).reshape(KV_RANK, N_HEADS * QK_PAD)
    wuvt = ukv[:, :, NOPE_DIM:].reshape(KV_RANK, ATTN_WIDTH).T
    row = lambda v: v.reshape(1, -1).astype(F32)
    return {
        "wab": _fold_call(w_in[:, :o0], _channel_dft_table()),
        "wlat": jnp.concatenate([w_in[:, o0:o2], lane_pad(wkr), lane_pad(_rotate_half_cols(wkr))], axis=1).astype(BF16),
        "wg": w_in[:, o3:].astype(BF16), "gq": row(g_q), "gkv": row(g_kv),
        "w1": w1.astype(BF16), "w2": w2.astype(BF16), "wuk": wuk.astype(BF16), "wuvt": wuvt.astype(BF16),
        "wf": w_fourier.astype(BF16), "wa": w_attn.astype(BF16), "wo": w_o.astype(BF16),
        "wgate": w_gate.astype(BF16), "wup": w_up.astype(BF16), "wdown": w_down.astype(BF16),
        "ln1g": row(ln1_g), "ln1b": row(ln1_b), "ln2g": row(ln2_g), "ln2b": row(ln2_b),
    }


def _encoder_layer(x, w, alpha):
    b, s, _ = x.shape
    n1, n2 = DFT_N1, s // DFT_N1
    m1, t = _dft_tables(s)
    ab, qt, k, vt, gates = _proj_call(x, w, _rope_tables(s))
    y = _dft1_call(ab.reshape(b, 2, n1, n2, FOURIER_WIDTH), m1)
    x3 = _dft2_call(y, t)
    ot = _attn_call(qt, k, vt)
    return _final_call(x, gates, x3.reshape(b, s, FOURIER_WIDTH), ot, w, alpha)


def kernel(x_prompt, x_sample, w_in, w_fourier, g_q, w_uq, g_kv, w_ukv, w_attn, w_o, ln1_g, ln1_b, w_gate, w_up, w_down, ln2_g, ln2_b):
    stacked = (w_in, w_fourier, g_q, w_uq, g_kv, w_ukv, w_attn, w_o, ln1_g, ln1_b, w_gate, w_up, w_down, ln2_g, ln2_b)
    depth = w_in.shape[0]
    alpha = (2.0 * depth) ** 0.25
    layer = lambda t, l: t.reshape(t.shape[1:]) if depth == 1 else t[l]
    layers = [_layer_weights(*(layer(t, l) for t in stacked)) for l in range(depth)]
    outs = []
    for x in (x_prompt, x_sample):
        for w in layers:
            x = _encoder_layer(x, w, alpha)
        outs.append(x)
    return tuple(outs)
```

```python
import functools
import math

import jax
import jax.numpy as jnp
import numpy as np
from jax import lax
from jax.experimental import pallas as pl
from jax.experimental.pallas import tpu as pltpu

F32 = jnp.float32
BF16 = jnp.bfloat16

D_MODEL = 1024
N_GROUPS = 4
GROUP_DIM = 128
FOURIER_WIDTH = N_GROUPS * GROUP_DIM
N_HEADS = 8
NOPE_DIM = 64
ROPE_DIM = 32
QK_DIM = NOPE_DIM + ROPE_DIM
V_DIM = 64
ATTN_WIDTH = N_HEADS * V_DIM
Q_RANK = 384
KV_RANK = 256
ROPE_THETA = 10000.0
D_FF = 2816
LN_EPS = 1e-5
RMS_EPS = 1e-6

VMEM_BYTES_V7X = 64 * 1024 * 1024
LANES = 128

QK_PAD = 128
V_PAD = 80
DFT_N1 = 128
TOK_TILE = 512
DFT1_ROWS = 16
DFT2_GROUP = 16
ATTN_TQ = 1024
ATTN_TK = 512
ATTN_UNROLL = 32
ATTN_INLINE_TILES = 4
ATTN_BIAS_ROW = QK_DIM
ATTN_BOUND_SLACK = 1.0 + 2.0 ** -6
ATTN_DENOM_MIN = 2.0 ** -60
FFN_CHUNKS = 2


def _params(semantics, vmem_mb):
    return pltpu.CompilerParams(dimension_semantics=semantics, vmem_limit_bytes=vmem_mb * 1024 * 1024)


def _const_spec(shape):
    zeros = (0,) * len(shape)
    return pl.BlockSpec(shape, lambda *_: zeros, pipeline_mode=pl.Buffered(1))


def _dot(a, b):
    return jnp.dot(a, b, preferred_element_type=F32)


def _dot_nt(a, b):
    return lax.dot_general(a, b, (((1,), (1,)), ((), ())), preferred_element_type=F32)


def _dot_tn(a, b):
    return lax.dot_general(a, b, (((0,), (0,)), ((), ())), preferred_element_type=F32)


def _fold_kernel(wf_ref, cs_ref, out_ref):
    for g in range(N_GROUPS):
        w = wf_ref[:, g * GROUP_DIM:(g + 1) * GROUP_DIM]
        r = jnp.dot(w, cs_ref[...], preferred_element_type=F32, precision=lax.Precision.HIGHEST)
        out_ref[:, g * GROUP_DIM:(g + 1) * GROUP_DIM] = r[:, :GROUP_DIM].astype(BF16)
        out_ref[:, FOURIER_WIDTH + g * GROUP_DIM:FOURIER_WIDTH + (g + 1) * GROUP_DIM] = r[:, GROUP_DIM:].astype(BF16)


def _fold_call(w_f, cs):
    return pl.pallas_call(
        _fold_kernel,
        out_shape=jax.ShapeDtypeStruct((D_MODEL, 2 * FOURIER_WIDTH), BF16),
        name="fold_channel_dft",
    )(w_f, cs)


def _proj_kernel(x_ref, wab_ref, wlat_ref, wg_ref, gq_ref, gkv_ref,
                 w1_ref, w2_ref, wuk_ref, wuvt_ref, cosq_ref, sinq_ref, cosk_ref, sink_ref,
                 ab_ref, qt_ref, k_ref, vt_ref, g_ref):
    xb = x_ref[0].astype(BF16)
    ab = _dot(xb, wab_ref[...])
    ab_ref[0, 0] = ab[:, :FOURIER_WIDTH].astype(BF16)
    ab_ref[0, 1] = ab[:, FOURIER_WIDTH:].astype(BF16)
    g_ref[0] = jax.nn.sigmoid(_dot(xb, wg_ref[...])).astype(BF16)

    o1 = Q_RANK
    o2 = o1 + KV_RANK
    o3 = o2 + QK_PAD
    lat = _dot(xb, wlat_ref[...])
    cq = lat[:, :o1]
    cqn = (cq * lax.rsqrt(jnp.mean(jnp.square(cq), axis=-1, keepdims=True) + RMS_EPS) * gq_ref[...]).astype(BF16)
    ckv = lat[:, o1:o2]
    ckvn = (ckv * lax.rsqrt(jnp.mean(jnp.square(ckv), axis=-1, keepdims=True) + RMS_EPS) * gkv_ref[...]).astype(BF16)

    kpe = lat[:, o2:o3] * cosk_ref[...] + lat[:, o3:] * sink_ref[...]
    lane = lax.broadcasted_iota(jnp.int32, (1, QK_PAD), 1)
    kpe = kpe + jnp.where(lane == ATTN_BIAS_ROW, 1.0, 0.0)
    kn = _dot(ckvn, wuk_ref[...])
    for h in range(N_HEADS):
        k_ref[0, h] = (kn[:, h * QK_PAD:(h + 1) * QK_PAD] + kpe).astype(BF16)

    qt = _dot_nt(w1_ref[...], cqn)
    qr = _dot_nt(w2_ref[...], cqn)
    cos = cosq_ref[...]
    sin = sinq_ref[...]
    for h in range(N_HEADS):
        r0 = h * QK_PAD
        qt_ref[0, h, 0:NOPE_DIM, :] = qt[r0:r0 + NOPE_DIM].astype(BF16)
        qt_ref[0, h, NOPE_DIM:QK_DIM, :] = (
            qt[r0 + NOPE_DIM:r0 + QK_DIM] * cos + qr[h * ROPE_DIM:(h + 1) * ROPE_DIM] * sin).astype(BF16)
        qt_ref[0, h, QK_DIM:QK_PAD, :] = jnp.zeros((QK_PAD - QK_DIM, qt.shape[1]), BF16)

    vt = _dot_nt(wuvt_ref[...], ckvn)
    for h in range(N_HEADS):
        vt_ref[0, h, 0:V_DIM, :] = vt[h * V_DIM:(h + 1) * V_DIM].astype(BF16)
        vt_ref[0, h, V_DIM:V_PAD, :] = jnp.ones((V_PAD - V_DIM, vt.shape[1]), BF16)


def _proj_call(x, w, tabs):
    b, s, _ = x.shape
    tm = TOK_TILE
    consts = [w["wab"], w["wlat"], w["wg"], w["gq"], w["gkv"], w["w1"], w["w2"], w["wuk"], w["wuvt"]]
    in_specs = [pl.BlockSpec((1, tm, D_MODEL), lambda bi, i: (bi, i, 0))]
    in_specs += [_const_spec(c.shape) for c in consts]
    in_specs += [
        pl.BlockSpec((ROPE_DIM, tm), lambda bi, i: (0, i)),
        pl.BlockSpec((ROPE_DIM, tm), lambda bi, i: (0, i)),
        pl.BlockSpec((tm, QK_PAD), lambda bi, i: (i, 0)),
        pl.BlockSpec((tm, QK_PAD), lambda bi, i: (i, 0)),
    ]
    out_shape = (
        jax.ShapeDtypeStruct((b, 2, s, FOURIER_WIDTH), BF16),
        jax.ShapeDtypeStruct((b, N_HEADS, QK_PAD, s), BF16),
        jax.ShapeDtypeStruct((b, N_HEADS, s, QK_PAD), BF16),
        jax.ShapeDtypeStruct((b, N_HEADS, V_PAD, s), BF16),
        jax.ShapeDtypeStruct((b, s, 2 * D_MODEL), BF16),
    )
    out_specs = (
        pl.BlockSpec((1, 2, tm, FOURIER_WIDTH), lambda bi, i: (bi, 0, i, 0)),
        pl.BlockSpec((1, N_HEADS, QK_PAD, tm), lambda bi, i: (bi, 0, 0, i)),
        pl.BlockSpec((1, N_HEADS, tm, QK_PAD), lambda bi, i: (bi, 0, i, 0)),
        pl.BlockSpec((1, N_HEADS, V_PAD, tm), lambda bi, i: (bi, 0, 0, i)),
        pl.BlockSpec((1, tm, 2 * D_MODEL), lambda bi, i: (bi, i, 0)),
    )
    return pl.pallas_call(
        _proj_kernel,
        grid=(b, s // tm),
        in_specs=in_specs,
        out_specs=out_specs,
        out_shape=out_shape,
        compiler_params=_params(("parallel", "parallel"), 56),
        name="proj",
    )(x, *consts, tabs["cosq"], tabs["sinq"], tabs["cosk"], tabs["sink"])


def _dft1_kernel(m1_ref, ab_ref, y_ref):
    _, _, n1, r, fw = ab_ref.shape
    zt = jnp.swapaxes(ab_ref[0].reshape(2 * n1, r, fw), 0, 1)
    m1 = m1_ref[...].astype(BF16)
    ys = [_dot(m1, zt[i]).astype(BF16) for i in range(r)]
    y_ref[0] = jnp.swapaxes(jnp.stack(ys, axis=0), 0, 1).reshape(2, n1, r, fw)


def _dft1_call(ab, m1):
    b, _, n1, n2, fw = ab.shape
    r = DFT1_ROWS
    return pl.pallas_call(
        _dft1_kernel,
        grid=(b, n2 // r),
        in_specs=[_const_spec(m1.shape), pl.BlockSpec((1, 2, n1, r, fw), lambda bi, i: (bi, 0, 0, i, 0))],
        out_specs=pl.BlockSpec((1, 2, n1, r, fw), lambda bi, i: (bi, 0, 0, i, 0)),
        out_shape=jax.ShapeDtypeStruct(ab.shape, BF16),
        compiler_params=_params(("parallel", "parallel"), 32),
        name="dft_stage1",
    )(m1, ab)


def _dft2_kernel(t_ref, y_ref, x_ref):
    xs = []
    for g in range(DFT2_GROUP):
        yk = jnp.concatenate([y_ref[0, 0, g], y_ref[0, 1, g]], axis=0)
        xs.append(_dot(t_ref[g].astype(BF16), yk).astype(BF16))
    x_ref[0] = jnp.swapaxes(jnp.stack(xs, axis=0), 0, 1)


def _dft2_call(y, t):
    b, _, n1, n2, fw = y.shape
    g = DFT2_GROUP
    return pl.pallas_call(
        _dft2_kernel,
        grid=(b, n1 // g),
        in_specs=[pl.BlockSpec((g, n2, 2 * n2), lambda bi, i: (i, 0, 0)),
                  pl.BlockSpec((1, 2, g, n2, fw), lambda bi, i: (bi, 0, i, 0, 0))],
        out_specs=pl.BlockSpec((1, n2, g, fw), lambda bi, i: (bi, 0, i, 0)),
        out_shape=jax.ShapeDtypeStruct((b, n2, n1, fw), BF16),
        compiler_params=_params(("parallel", "parallel"), 32),
        name="dft_stage2",
    )(t, y)


def _attn_kernel(qt_ref, k_ref, vt_ref, o_ref, qa_ref, m_ref, acc_ref):
    s_len = k_ref.shape[2]
    tq, tk = qa_ref.shape[1], ATTN_TK
    nblk = s_len // tk
    unroll = min(ATTN_UNROLL, nblk)

    def kv_block(j):
        koff = j * tk if isinstance(j, int) else pl.multiple_of(j * tk, tk)
        return k_ref[0, 0, pl.ds(koff, tk), :], vt_ref[0, 0, :, pl.ds(koff, tk)]

    ones = jnp.ones((16, QK_PAD), BF16)

    def key_norm(i, mx):
        for u in range(unroll):
            kf = kv_block(i * unroll + u)[0].astype(F32)
            mx = jnp.maximum(mx, _dot_nt(ones, (kf * kf).astype(BF16)))
        return mx

    ksq = jnp.max(lax.fori_loop(0, nblk // unroll, key_norm, jnp.zeros((16, tk), F32)))
    row = lax.broadcasted_iota(jnp.int32, (QK_PAD, tq), 0)

    n_qt = s_len // tq
    trips_per_qt = nblk // unroll

    def tile_offset(qi):
        return qi * tq if isinstance(qi, int) else pl.multiple_of(qi * tq, tq)

    def write_tile(qi, acc):
        o_ref[0, :, pl.ds(tile_offset(qi), tq)] = (acc[:V_DIM] / acc[V_DIM:V_DIM + 1]).astype(BF16)

    def needs_exact(acc):
        return jnp.logical_not(jnp.min(acc[V_DIM:V_DIM + 1]) >= ATTN_DENOM_MIN)

    def fast_tile(qi):
        qf = qt_ref[0, 0, :, pl.ds(tile_offset(qi), tq)].astype(F32)
        bound = jnp.sqrt(jnp.sum(qf * qf, axis=0, keepdims=True) * ksq) * ATTN_BOUND_SLACK
        qa_ref[...] = jnp.where(row == ATTN_BIAS_ROW, -bound, qf).astype(BF16)

        def trip(t):
            part = None
            sc = _dot(kv_block(t * unroll)[0], qa_ref[...])
            for u in range(unroll):
                sc_next = _dot(kv_block(t * unroll + u + 1)[0], qa_ref[...]) if u + 1 < unroll else None
                pv = _dot(kv_block(t * unroll + u)[1], jnp.exp2(sc).astype(BF16))
                part = pv if part is None else part + pv
                sc = sc_next
            return part

        if trips_per_qt == 1:
            return trip(0)
        acc_ref[...] = jnp.zeros(acc_ref.shape, F32)

        def add_trip(t, c):
            acc_ref[...] += trip(t)
            return c

        lax.fori_loop(0, trips_per_qt, add_trip, 0)
        return acc_ref[...]

    def exact_tile(qi):
        qt = qt_ref[0, 0, :, pl.ds(tile_offset(qi), tq)]
        m_ref[...] = jnp.full(m_ref.shape, -1e30, F32)
        acc_ref[...] = jnp.zeros(acc_ref.shape, F32)

        def exact_block(j, c):
            kb, vb = kv_block(j)
            sc = _dot(kb, qt)
            m_old = m_ref[...]
            m_new = jnp.maximum(m_old, jnp.max(sc, axis=0, keepdims=True))
            acc_ref[...] = acc_ref[...] * jnp.exp2(m_old - m_new) + _dot(vb, jnp.exp2(sc - m_new).astype(BF16))
            m_ref[...] = m_new
            return c

        lax.fori_loop(0, nblk, exact_block, 0)
        write_tile(qi, acc_ref[...])

    if trips_per_qt == 1 and n_qt <= ATTN_INLINE_TILES:
        redo = []
        for qi in range(n_qt):
            acc = fast_tile(qi)
            write_tile(qi, acc)
            redo.append(needs_exact(acc))
        for qi in range(n_qt):
            pl.when(redo[qi])(functools.partial(exact_tile, qi))
    else:
        def q_tile(qi, carry):
            acc = fast_tile(qi)
            write_tile(qi, acc)
            pl.when(needs_exact(acc))(functools.partial(exact_tile, qi))
            return carry

        lax.fori_loop(0, n_qt, q_tile, 0)


def _attn_call(qt, k, vt):
    b, h, _, s = qt.shape
    tq = ATTN_TQ
    return pl.pallas_call(
        _attn_kernel,
        grid=(b, h),
        in_specs=[pl.BlockSpec((1, 1, QK_PAD, s), lambda bi, hi: (bi, hi, 0, 0)),
                  pl.BlockSpec((1, 1, s, QK_PAD), lambda bi, hi: (bi, hi, 0, 0)),
                  pl.BlockSpec((1, 1, V_PAD, s), lambda bi, hi: (bi, hi, 0, 0))],
        out_specs=pl.BlockSpec((1, V_DIM, s), lambda bi, hi: (bi, hi, 0)),
        out_shape=jax.ShapeDtypeStruct((b, ATTN_WIDTH, s), BF16),
        scratch_shapes=[pltpu.VMEM((QK_PAD, tq), BF16), pltpu.VMEM((1, tq), F32), pltpu.VMEM((V_PAD, tq), F32)],
        compiler_params=_params(("parallel", "parallel"), 48),
        name="attention",
    )(qt, k, vt)


def _layer_norm(v, g, b):
    mu = jnp.mean(v, axis=-1, keepdims=True)
    var = jnp.mean(jnp.square(v - mu), axis=-1, keepdims=True)
    return (v - mu) * lax.rsqrt(var + LN_EPS) * g + b


def _final_kernel(alpha, x_ref, g_ref, f_ref, ot_ref, wf_ref, wa_ref, wo_ref, wgate_ref, wup_ref, wdown_ref,
                  ln1g_ref, ln1b_ref, ln2g_ref, ln2b_ref, y_ref):
    branch_a = _dot(f_ref[0], wf_ref[...])
    branch_b = _dot_tn(ot_ref[0], wa_ref[...])
    g = g_ref[0]
    merged = (g[:, :D_MODEL].astype(F32) * branch_a + g[:, D_MODEL:].astype(F32) * branch_b).astype(BF16)
    h = _layer_norm(alpha * x_ref[0] + _dot(merged, wo_ref[...]), ln1g_ref[...], ln1b_ref[...])
    hb = h.astype(BF16)
    ck = D_FF // FFN_CHUNKS
    ffn = None
    for c in range(FFN_CHUNKS):
        gate = _dot(hb, wgate_ref[:, c * ck:(c + 1) * ck])
        up = _dot(hb, wup_ref[:, c * ck:(c + 1) * ck])
        part = _dot((jax.nn.silu(gate) * up).astype(BF16), wdown_ref[c * ck:(c + 1) * ck, :])
        ffn = part if ffn is None else ffn + part
    y_ref[0] = _layer_norm(alpha * h + ffn, ln2g_ref[...], ln2b_ref[...])


def _final_call(x, gates, x3, ot, w, alpha):
    b, s, _ = x.shape
    tm = TOK_TILE
    consts = [w["wf"], w["wa"], w["wo"], w["wgate"], w["wup"], w["wdown"],
              w["ln1g"], w["ln1b"], w["ln2g"], w["ln2b"]]
    in_specs = [pl.BlockSpec((1, tm, D_MODEL), lambda bi, i: (bi, i, 0)),
                pl.BlockSpec((1, tm, 2 * D_MODEL), lambda bi, i: (bi, i, 0)),
                pl.BlockSpec((1, tm, FOURIER_WIDTH), lambda bi, i: (bi, i, 0)),
                pl.BlockSpec((1, ATTN_WIDTH, tm), lambda bi, i: (bi, 0, i))]
    in_specs += [_const_spec(c.shape) for c in consts]
    return pl.pallas_call(
        functools.partial(_final_kernel, alpha),
        grid=(b, s // tm),
        in_specs=in_specs,
        out_specs=pl.BlockSpec((1, tm, D_MODEL), lambda bi, i: (bi, i, 0)),
        out_shape=jax.ShapeDtypeStruct((b, s, D_MODEL), F32),
        compiler_params=_params(("parallel", "parallel"), 56),
        name="merge_ffn",
    )(x, gates, x3, ot, *consts)


def _cos_sin(num, den):
    ang = (2.0 * np.pi / den) * (num % den).astype(np.float64)
    return np.cos(ang), np.sin(ang)


def _channel_dft_table():
    j = np.arange(GROUP_DIM)
    c, s = _cos_sin(j[:, None] * j[None, :], GROUP_DIM)
    return (np.concatenate([c, s], axis=1) * GROUP_DIM ** -0.5).astype(np.float32)


def _dft_tables(s_len):
    n1, n2 = DFT_N1, s_len // DFT_N1
    i1 = np.arange(n1)
    c, s = _cos_sin(i1[:, None] * i1[None, :], n1)
    m1 = np.block([[c, -s], [-s, -c]]) * n1 ** -0.5
    i2 = np.arange(n2)
    k = i1[:, None, None] + n1 * i2[None, :, None]
    tc, ts = _cos_sin(k * i2[None, None, :], s_len)
    t = np.concatenate([tc, ts], axis=2) * n2 ** -0.5
    return m1.astype(np.float32), t.astype(np.float32)


def _rope_tables(s_len):
    f32 = np.float32
    inv_freq = 1.0 / ROPE_THETA ** (np.arange(0, ROPE_DIM, 2, dtype=np.float64) / ROPE_DIM)
    ang = np.arange(s_len, dtype=np.float64)[:, None] * inv_freq[None, :]
    cos2 = np.concatenate([np.cos(ang)] * 2, axis=1).astype(f32)
    sin2 = np.concatenate([np.sin(ang)] * 2, axis=1).astype(f32)
    pad = lambda t: np.pad(t, ((0, 0), (NOPE_DIM, QK_PAD - QK_DIM)))
    return {"cosq": np.ascontiguousarray(cos2.T), "sinq": np.ascontiguousarray(sin2.T),
            "cosk": pad(cos2), "sink": pad(sin2)}


def _rotate_half_cols(w):
    half = ROPE_DIM // 2
    return jnp.concatenate([-w[..., half:], w[..., :half]], axis=-1)


def _layer_weights(w_in, w_fourier, g_q, w_uq, g_kv, w_ukv, w_attn, w_o,
                   ln1_g, ln1_b, w_gate, w_up, w_down, ln2_g, ln2_b):
    o0 = FOURIER_WIDTH
    o1 = o0 + Q_RANK
    o2 = o1 + KV_RANK
    o3 = o2 + ROPE_DIM
    lane_pad = lambda t: jnp.pad(t, ((0, 0), (NOPE_DIM, QK_PAD - QK_DIM)))
    wkr = w_in[:, o2:o3]
    q_scale = (QK_DIM ** -0.5) * math.log2(math.e)
    uq = (w_uq * q_scale).reshape(Q_RANK, N_HEADS, QK_DIM)
    w1 = jnp.pad(uq, ((0, 0), (0, 0), (0, QK_PAD - QK_DIM))).reshape(Q_RANK, N_HEADS * QK_PAD).T
    w2 = _rotate_half_cols(uq[:, :, NOPE_DIM:]).reshape(Q_RANK, N_HEADS * ROPE_DIM).T
    ukv = w_ukv.reshape(KV_RANK, N_HEADS, NOPE_DIM + V_DIM)
    wuk = jnp.pad(ukv[:, :, :NOPE_DIM], ((0, 0), (0, 0), (0, QK_PAD - NOPE_DIM))).reshape(KV_RANK, N_HEADS * QK_PAD)
    wuvt = ukv[:, :, NOPE_DIM:].reshape(KV_RANK, ATTN_WIDTH).T
    row = lambda v: v.reshape(1, -1).astype(F32)
    return {
        "wab": _fold_call(w_in[:, :o0], _channel_dft_table()),
        "wlat": jnp.concatenate([w_in[:, o0:o2], lane_pad(wkr), lane_pad(_rotate_half_cols(wkr))], axis=1).astype(BF16),
        "wg": w_in[:, o3:].astype(BF16), "gq": row(g_q), "gkv": row(g_kv),
        "w1": w1.astype(BF16), "w2": w2.astype(BF16), "wuk": wuk.astype(BF16), "wuvt": wuvt.astype(BF16),
        "wf": w_fourier.astype(BF16), "wa": w_attn.astype(BF16), "wo": w_o.astype(BF16),
        "wgate": w_gate.astype(BF16), "wup": w_up.astype(BF16), "wdown": w_down.astype(BF16),
        "ln1g": row(ln1_g), "ln1b": row(ln1_b), "ln2g": row(ln2_g), "ln2b": row(ln2_b),
    }


def _encoder_layer(x, w, alpha):
    b, s, _ = x.shape
    n1, n2 = DFT_N1, s // DFT_N1
    m1, t = _dft_tables(s)
    ab, qt, k, vt, gates = _proj_call(x, w, _rope_tables(s))
    y = _dft1_call(ab.reshape(b, 2, n1, n2, FOURIER_WIDTH), m1)
    x3 = _dft2_call(y, t)
    ot = _attn_call(qt, k, vt)
    return _final_call(x, gates, x3.reshape(b, s, FOURIER_WIDTH), ot, w, alpha)


def kernel(x_prompt, x_sample, w_in, w_fourier, g_q, w_uq, g_kv, w_ukv, w_attn, w_o, ln1_g, ln1_b, w_gate, w_up, w_down, ln2_g, ln2_b):
    stacked = (w_in, w_fourier, g_q, w_uq, g_kv, w_ukv, w_attn, w_o, ln1_g, ln1_b, w_gate, w_up, w_down, ln2_g, ln2_b)
    depth = w_in.shape[0]
    alpha = (2.0 * depth) ** 0.25
    layer = lambda t, l: t.reshape(t.shape[1:]) if depth == 1 else t[l]
    layers = [_layer_weights(*(layer(t, l) for t in stacked)) for l in range(depth)]
    outs = []
    for x in (x_prompt, x_sample):
        for w in layers:
            x = _encoder_layer(x, w, alpha)
        outs.append(x)
    return tuple(outs)
```

```python
import functools
import math

import jax
import jax.numpy as jnp
import numpy as np
from jax import lax
from jax.experimental import pallas as pl
from jax.experimental.pallas import tpu as pltpu

F32 = jnp.float32
BF16 = jnp.bfloat16

D_MODEL = 1024
N_GROUPS = 4
GROUP_DIM = 128
FOURIER_WIDTH = N_GROUPS * GROUP_DIM
N_HEADS = 8
NOPE_DIM = 64
ROPE_DIM = 32
QK_DIM = NOPE_DIM + ROPE_DIM
V_DIM = 64
ATTN_WIDTH = N_HEADS * V_DIM
Q_RANK = 384
KV_RANK = 256
ROPE_THETA = 10000.0
D_FF = 2816
LN_EPS = 1e-5
RMS_EPS = 1e-6

VMEM_BYTES_V7X = 64 * 1024 * 1024
LANES = 128

QK_PAD = 128
V_PAD = 80
DFT_N1 = 128
TOK_TILE = 512
DFT1_ROWS = 16
DFT2_GROUP = 16
ATTN_TQ = 1024
ATTN_TK = 512
ATTN_UNROLL = 32
ATTN_INLINE_TILES = 4
ATTN_BIAS_ROW = QK_DIM
ATTN_BOUND_SLACK = 1.0 + 2.0 ** -6
ATTN_DENOM_MIN = 2.0 ** -60
ROW_GROUPS = 2


def _params(semantics, vmem_mb):
    return pltpu.CompilerParams(dimension_semantics=semantics, vmem_limit_bytes=vmem_mb * 1024 * 1024)


def _const_spec(shape):
    zeros = (0,) * len(shape)
    return pl.BlockSpec(shape, lambda *_: zeros, pipeline_mode=pl.Buffered(1))


def _dot(a, b):
    return jnp.dot(a, b, preferred_element_type=F32)


def _dot_nt(a, b):
    return lax.dot_general(a, b, (((1,), (1,)), ((), ())), preferred_element_type=F32)


def _dot_tn(a, b):
    return lax.dot_general(a, b, (((0,), (0,)), ((), ())), preferred_element_type=F32)


def _fold_kernel(wf_ref, cs_ref, out_ref):
    for g in range(N_GROUPS):
        w = wf_ref[:, g * GROUP_DIM:(g + 1) * GROUP_DIM]
        r = jnp.dot(w, cs_ref[...], preferred_element_type=F32, precision=lax.Precision.HIGHEST)
        out_ref[:, g * GROUP_DIM:(g + 1) * GROUP_DIM] = r[:, :GROUP_DIM].astype(BF16)
        out_ref[:, FOURIER_WIDTH + g * GROUP_DIM:FOURIER_WIDTH + (g + 1) * GROUP_DIM] = r[:, GROUP_DIM:].astype(BF16)


def _fold_call(w_f, cs):
    return pl.pallas_call(
        _fold_kernel,
        out_shape=jax.ShapeDtypeStruct((D_MODEL, 2 * FOURIER_WIDTH), BF16),
        name="fold_channel_dft",
    )(w_f, cs)


def _proj_kernel(x_ref, wab_ref, wlat_ref, wg_ref, gq_ref, gkv_ref,
                 w1_ref, w2_ref, wuk_ref, wuvt_ref, cosq_ref, sinq_ref, cosk_ref, sink_ref,
                 ab_ref, qt_ref, k_ref, vt_ref, g_ref):
    xb = x_ref[0].astype(BF16)
    o1 = Q_RANK
    o2 = o1 + KV_RANK
    lat = _dot(xb, wlat_ref[...])
    ab = _dot(xb, wab_ref[...])
    ab_ref[0, 0] = ab[:, :FOURIER_WIDTH].astype(BF16)
    ab_ref[0, 1] = ab[:, FOURIER_WIDTH:].astype(BF16)
    g_ref[0] = jax.nn.sigmoid(_dot(xb, wg_ref[...])).astype(BF16)

    cq = lat[:, :o1]
    cqn = (cq * lax.rsqrt(jnp.mean(jnp.square(cq), axis=-1, keepdims=True) + RMS_EPS) * gq_ref[...]).astype(BF16)
    ckv = lat[:, o1:o2]
    ckvn = (ckv * lax.rsqrt(jnp.mean(jnp.square(ckv), axis=-1, keepdims=True) + RMS_EPS) * gkv_ref[...]).astype(BF16)

    kraw = lat[:, o2:]
    kpe = kraw * cosk_ref[...] + pltpu.roll(kraw, QK_PAD // 2, axis=1) * sink_ref[...]
    lane = lax.broadcasted_iota(jnp.int32, (1, QK_PAD), 1)
    kpe = kpe + jnp.where(lane == ATTN_BIAS_ROW, 1.0, 0.0)
    kn = _dot(ckvn, wuk_ref[...])
    for h in range(N_HEADS):
        k_ref[0, h] = (kn[:, h * QK_PAD:(h + 1) * QK_PAD] + kpe).astype(BF16)

    qt = _dot_nt(w1_ref[...], cqn)
    qr = _dot_nt(w2_ref[...], cqn)
    cos = cosq_ref[...]
    sin = sinq_ref[...]
    for h in range(N_HEADS):
        r0 = h * QK_DIM
        qt_ref[0, h, 0:NOPE_DIM, :] = qt[r0:r0 + NOPE_DIM].astype(BF16)
        qt_ref[0, h, NOPE_DIM:QK_DIM, :] = (
            qt[r0 + NOPE_DIM:r0 + QK_DIM] * cos + qr[h * ROPE_DIM:(h + 1) * ROPE_DIM] * sin).astype(BF16)
        qt_ref[0, h, QK_DIM:QK_PAD, :] = jnp.zeros((QK_PAD - QK_DIM, qt.shape[1]), BF16)

    vt = _dot_nt(wuvt_ref[...], ckvn)
    for h in range(N_HEADS):
        vt_ref[0, h, 0:V_DIM, :] = vt[h * V_DIM:(h + 1) * V_DIM].astype(BF16)
        vt_ref[0, h, V_DIM:V_PAD, :] = jnp.ones((V_PAD - V_DIM, vt.shape[1]), BF16)


def _proj_call(x, w, tabs):
    b, s, _ = x.shape
    tm = TOK_TILE
    consts = [w["wab"], w["wlat"], w["wg"], w["gq"], w["gkv"], w["w1"], w["w2"], w["wuk"], w["wuvt"]]
    in_specs = [pl.BlockSpec((1, tm, D_MODEL), lambda bi, i: (bi, i, 0))]
    in_specs += [_const_spec(c.shape) for c in consts]
    in_specs += [
        pl.BlockSpec((ROPE_DIM, tm), lambda bi, i: (0, i)),
        pl.BlockSpec((ROPE_DIM, tm), lambda bi, i: (0, i)),
        pl.BlockSpec((tm, QK_PAD), lambda bi, i: (i, 0)),
        pl.BlockSpec((tm, QK_PAD), lambda bi, i: (i, 0)),
    ]
    out_shape = (
        jax.ShapeDtypeStruct((b, 2, s, FOURIER_WIDTH), BF16),
        jax.ShapeDtypeStruct((b, N_HEADS, QK_PAD, s), BF16),
        jax.ShapeDtypeStruct((b, N_HEADS, s, QK_PAD), BF16),
        jax.ShapeDtypeStruct((b, N_HEADS, V_PAD, s), BF16),
        jax.ShapeDtypeStruct((b, s, 2 * D_MODEL), BF16),
    )
    out_specs = (
        pl.BlockSpec((1, 2, tm, FOURIER_WIDTH), lambda bi, i: (bi, 0, i, 0)),
        pl.BlockSpec((1, N_HEADS, QK_PAD, tm), lambda bi, i: (bi, 0, 0, i)),
        pl.BlockSpec((1, N_HEADS, tm, QK_PAD), lambda bi, i: (bi, 0, i, 0)),
        pl.BlockSpec((1, N_HEADS, V_PAD, tm), lambda bi, i: (bi, 0, 0, i)),
        pl.BlockSpec((1, tm, 2 * D_MODEL), lambda bi, i: (bi, i, 0)),
    )
    return pl.pallas_call(
        _proj_kernel,
        grid=(b, s // tm),
        in_specs=in_specs,
        out_specs=out_specs,
        out_shape=out_shape,
        compiler_params=_params(("parallel", "parallel"), 56),
        name="proj",
    )(x, *consts, tabs["cosq"], tabs["sinq"], tabs["cosk"], tabs["sink"])


def _dft1_kernel(m1_ref, ab_ref, y_ref):
    _, _, n1, r, fw = ab_ref.shape
    zt = jnp.swapaxes(ab_ref[0].reshape(2 * n1, r, fw), 0, 1)
    m1 = m1_ref[...].astype(BF16)
    ys = [_dot(m1, zt[i]).astype(BF16) for i in range(r)]
    y_ref[0] = jnp.swapaxes(jnp.stack(ys, axis=0), 0, 1).reshape(2, n1, r, fw)


def _dft1_call(ab, m1):
    b, _, n1, n2, fw = ab.shape
    r = DFT1_ROWS
    return pl.pallas_call(
        _dft1_kernel,
        grid=(b, n2 // r),
        in_specs=[_const_spec(m1.shape), pl.BlockSpec((1, 2, n1, r, fw), lambda bi, i: (bi, 0, 0, i, 0))],
        out_specs=pl.BlockSpec((1, 2, n1, r, fw), lambda bi, i: (bi, 0, 0, i, 0)),
        out_shape=jax.ShapeDtypeStruct(ab.shape, BF16),
        compiler_params=_params(("parallel", "parallel"), 32),
        name="dft_stage1",
    )(m1, ab)


def _dft2_kernel(t_ref, y_ref, x_ref):
    xs = []
    for g in range(DFT2_GROUP):
        yk = jnp.concatenate([y_ref[0, 0, g], y_ref[0, 1, g]], axis=0)
        xs.append(_dot(t_ref[g].astype(BF16), yk).astype(BF16))
    x_ref[0] = jnp.swapaxes(jnp.stack(xs, axis=0), 0, 1)


def _dft2_call(y, t):
    b, _, n1, n2, fw = y.shape
    g = DFT2_GROUP
    return pl.pallas_call(
        _dft2_kernel,
        grid=(b, n1 // g),
        in_specs=[pl.BlockSpec((g, n2, 2 * n2), lambda bi, i: (i, 0, 0)),
                  pl.BlockSpec((1, 2, g, n2, fw), lambda bi, i: (bi, 0, i, 0, 0))],
        out_specs=pl.BlockSpec((1, n2, g, fw), lambda bi, i: (bi, 0, i, 0)),
        out_shape=jax.ShapeDtypeStruct((b, n2, n1, fw), BF16),
        compiler_params=_params(("parallel", "parallel"), 32),
        name="dft_stage2",
    )(t, y)


def _attn_kernel(qt_ref, k_ref, vt_ref, o_ref, qa_ref, m_ref, acc_ref):
    s_len = k_ref.shape[2]
    tq, tk = qa_ref.shape[1], ATTN_TK
    nblk = s_len // tk
    unroll = min(ATTN_UNROLL, nblk)

    def kv_block(j):
        koff = j * tk if isinstance(j, int) else pl.multiple_of(j * tk, tk)
        return k_ref[0, 0, pl.ds(koff, tk), :], vt_ref[0, 0, :, pl.ds(koff, tk)]

    ones = jnp.ones((16, QK_PAD), BF16)

    def key_norm(i, mx):
        for u in range(unroll):
            kf = kv_block(i * unroll + u)[0].astype(F32)
            mx = jnp.maximum(mx, _dot_nt(ones, (kf * kf).astype(BF16)))
        return mx

    ksq = jnp.max(lax.fori_loop(0, nblk // unroll, key_norm, jnp.zeros((16, tk), F32)))
    row = lax.broadcasted_iota(jnp.int32, (QK_PAD, tq), 0)

    n_qt = s_len // tq
    trips_per_qt = nblk // unroll

    def tile_offset(qi):
        return qi * tq if isinstance(qi, int) else pl.multiple_of(qi * tq, tq)

    def write_tile(qi, acc):
        o_ref[0, :, pl.ds(tile_offset(qi), tq)] = (acc[:V_DIM] / acc[V_DIM:V_DIM + 1]).astype(BF16)

    def needs_exact(acc):
        return jnp.logical_not(jnp.min(acc[V_DIM:V_DIM + 1]) >= ATTN_DENOM_MIN)

    def fast_tile(qi):
        qf = qt_ref[0, 0, :, pl.ds(tile_offset(qi), tq)].astype(F32)
        bound = jnp.sqrt(jnp.sum(qf * qf, axis=0, keepdims=True) * ksq) * ATTN_BOUND_SLACK
        qa_ref[...] = jnp.where(row == ATTN_BIAS_ROW, -bound, qf).astype(BF16)

        def trip(t):
            part = None
            sc = _dot(kv_block(t * unroll)[0], qa_ref[...])
            for u in range(unroll):
                sc_next = _dot(kv_block(t * unroll + u + 1)[0], qa_ref[...]) if u + 1 < unroll else None
                pv = _dot(kv_block(t * unroll + u)[1], jnp.exp2(sc).astype(BF16))
                part = pv if part is None else part + pv
                sc = sc_next
            return part

        if trips_per_qt == 1:
            return trip(0)
        acc_ref[...] = jnp.zeros(acc_ref.shape, F32)

        def add_trip(t, c):
            acc_ref[...] += trip(t)
            return c

        lax.fori_loop(0, trips_per_qt, add_trip, 0)
        return acc_ref[...]

    def exact_tile(qi):
        qt = qt_ref[0, 0, :, pl.ds(tile_offset(qi), tq)]
        m_ref[...] = jnp.full(m_ref.shape, -1e30, F32)
        acc_ref[...] = jnp.zeros(acc_ref.shape, F32)

        def exact_block(j, c):
            kb, vb = kv_block(j)
            sc = _dot(kb, qt)
            m_old = m_ref[...]
            m_new = jnp.maximum(m_old, jnp.max(sc, axis=0, keepdims=True))
            acc_ref[...] = acc_ref[...] * jnp.exp2(m_old - m_new) + _dot(vb, jnp.exp2(sc - m_new).astype(BF16))
            m_ref[...] = m_new
            return c

        lax.fori_loop(0, nblk, exact_block, 0)
        write_tile(qi, acc_ref[...])

    if trips_per_qt == 1 and n_qt <= ATTN_INLINE_TILES:
        redo = []
        for qi in range(n_qt):
            acc = fast_tile(qi)
            write_tile(qi, acc)
            redo.append(needs_exact(acc))
        for qi in range(n_qt):
            pl.when(redo[qi])(functools.partial(exact_tile, qi))
    else:
        def q_tile(qi, carry):
            acc = fast_tile(qi)
            write_tile(qi, acc)
            pl.when(needs_exact(acc))(functools.partial(exact_tile, qi))
            return carry

        lax.fori_loop(0, n_qt, q_tile, 0)


def _attn_call(qt, k, vt):
    b, h, _, s = qt.shape
    tq = ATTN_TQ
    return pl.pallas_call(
        _attn_kernel,
        grid=(b, h),
        in_specs=[pl.BlockSpec((1, 1, QK_PAD, s), lambda bi, hi: (bi, hi, 0, 0)),
                  pl.BlockSpec((1, 1, s, QK_PAD), lambda bi, hi: (bi, hi, 0, 0)),
                  pl.BlockSpec((1, 1, V_PAD, s), lambda bi, hi: (bi, hi, 0, 0))],
        out_specs=pl.BlockSpec((1, V_DIM, s), lambda bi, hi: (bi, hi, 0)),
        out_shape=jax.ShapeDtypeStruct((b, ATTN_WIDTH, s), BF16),
        scratch_shapes=[pltpu.VMEM((QK_PAD, tq), BF16), pltpu.VMEM((1, tq), F32), pltpu.VMEM((V_PAD, tq), F32)],
        compiler_params=_params(("parallel", "parallel"), 48),
        name="attention",
    )(qt, k, vt)


def _layer_norm(v, g, b):
    mu = jnp.mean(v, axis=-1, keepdims=True)
    var = jnp.mean(jnp.square(v - mu), axis=-1, keepdims=True)
    return (v - mu) * lax.rsqrt(var + LN_EPS) * g + b


def _final_kernel(alpha, x_ref, g_ref, f_ref, ot_ref, wf_ref, wa_ref, wo_ref, wgate_ref, wup_ref, wdown_ref,
                  ln1g_ref, ln1b_ref, ln2g_ref, ln2b_ref, y_ref):
    rows = x_ref.shape[1] // ROW_GROUPS
    groups = [slice(r * rows, (r + 1) * rows) for r in range(ROW_GROUPS)]
    mix = []
    for rs in groups:
        branch_a = _dot(f_ref[0, rs], wf_ref[...])
        branch_b = _dot_tn(ot_ref[0, :, rs], wa_ref[...])
        g = g_ref[0, rs]
        merged = (g[:, :D_MODEL].astype(F32) * branch_a + g[:, D_MODEL:].astype(F32) * branch_b).astype(BF16)
        mix.append(_dot(merged, wo_ref[...]))
    hs = [_layer_norm(alpha * x_ref[0, rs] + m, ln1g_ref[...], ln1b_ref[...]) for rs, m in zip(groups, mix)]
    acts = []
    for h in hs:
        hb = h.astype(BF16)
        acts.append((jax.nn.silu(_dot(hb, wgate_ref[...])) * _dot(hb, wup_ref[...])).astype(BF16))
    ffns = [_dot(a, wdown_ref[...]) for a in acts]
    for rs, h, ffn in zip(groups, hs, ffns):
        y_ref[0, rs] = _layer_norm(alpha * h + ffn, ln2g_ref[...], ln2b_ref[...])


def _final_call(x, gates, x3, ot, w, alpha):
    b, s, _ = x.shape
    tm = TOK_TILE
    consts = [w["wf"], w["wa"], w["wo"], w["wgate"], w["wup"], w["wdown"],
              w["ln1g"], w["ln1b"], w["ln2g"], w["ln2b"]]
    in_specs = [pl.BlockSpec((1, tm, D_MODEL), lambda bi, i: (bi, i, 0)),
                pl.BlockSpec((1, tm, 2 * D_MODEL), lambda bi, i: (bi, i, 0)),
                pl.BlockSpec((1, tm, FOURIER_WIDTH), lambda bi, i: (bi, i, 0)),
                pl.BlockSpec((1, ATTN_WIDTH, tm), lambda bi, i: (bi, 0, i))]
    in_specs += [_const_spec(c.shape) for c in consts]
    return pl.pallas_call(
        functools.partial(_final_kernel, alpha),
        grid=(b, s // tm),
        in_specs=in_specs,
        out_specs=pl.BlockSpec((1, tm, D_MODEL), lambda bi, i: (bi, i, 0)),
        out_shape=jax.ShapeDtypeStruct((b, s, D_MODEL), F32),
        compiler_params=_params(("parallel", "parallel"), 56),
        name="merge_ffn",
    )(x, gates, x3, ot, *consts)


def _cos_sin(num, den):
    ang = (2.0 * np.pi / den) * (num % den).astype(np.float64)
    return np.cos(ang), np.sin(ang)


def _channel_dft_table():
    j = np.arange(GROUP_DIM)
    c, s = _cos_sin(j[:, None] * j[None, :], GROUP_DIM)
    return (np.concatenate([c, s], axis=1) * GROUP_DIM ** -0.5).astype(np.float32)


def _dft_tables(s_len):
    n1, n2 = DFT_N1, s_len // DFT_N1
    i1 = np.arange(n1)
    c, s = _cos_sin(i1[:, None] * i1[None, :], n1)
    m1 = np.block([[c, -s], [-s, -c]]) * n1 ** -0.5
    i2 = np.arange(n2)
    k = i1[:, None, None] + n1 * i2[None, :, None]
    tc, ts = _cos_sin(k * i2[None, None, :], s_len)
    t = np.concatenate([tc, ts], axis=2) * n2 ** -0.5
    return m1.astype(np.float32), t.astype(np.float32)


def _rope_tables(s_len):
    f32 = np.float32
    inv_freq = 1.0 / ROPE_THETA ** (np.arange(0, ROPE_DIM, 2, dtype=np.float64) / ROPE_DIM)
    ang = np.arange(s_len, dtype=np.float64)[:, None] * inv_freq[None, :]
    cos2 = np.concatenate([np.cos(ang)] * 2, axis=1).astype(f32)
    sin2 = np.concatenate([np.sin(ang)] * 2, axis=1).astype(f32)
    pad = lambda t: np.pad(t, ((0, 0), (NOPE_DIM, QK_PAD - QK_DIM)))
    return {"cosq": np.ascontiguousarray(cos2.T), "sinq": np.ascontiguousarray(sin2.T),
            "cosk": pad(cos2), "sink": pad(sin2)}


def _rotate_half_cols(w):
    half = ROPE_DIM // 2
    return jnp.concatenate([-w[..., half:], w[..., :half]], axis=-1)


def _layer_weights(w_in, w_fourier, g_q, w_uq, g_kv, w_ukv, w_attn, w_o,
                   ln1_g, ln1_b, w_gate, w_up, w_down, ln2_g, ln2_b):
    o0 = FOURIER_WIDTH
    o1 = o0 + Q_RANK
    o2 = o1 + KV_RANK
    o3 = o2 + ROPE_DIM
    wkr = w_in[:, o2:o3]
    q_scale = (QK_DIM ** -0.5) * math.log2(math.e)
    uq = (w_uq * q_scale).reshape(Q_RANK, N_HEADS, QK_DIM)
    w1 = uq.reshape(Q_RANK, N_HEADS * QK_DIM).T
    w2 = _rotate_half_cols(uq[:, :, NOPE_DIM:]).reshape(Q_RANK, N_HEADS * ROPE_DIM).T
    ukv = w_ukv.reshape(KV_RANK, N_HEADS, NOPE_DIM + V_DIM)
    wuk = jnp.pad(ukv[:, :, :NOPE_DIM], ((0, 0), (0, 0), (0, QK_PAD - NOPE_DIM))).reshape(KV_RANK, N_HEADS * QK_PAD)
    wuvt = ukv[:, :, NOPE_DIM:].reshape(KV_RANK, ATTN_WIDTH).T
    row = lambda v: v.reshape(1, -1).astype(F32)
    return {
        "wab": _fold_call(w_in[:, :o0], _channel_dft_table()),
        "wlat": jnp.concatenate([w_in[:, o0:o2], _rotate_half_cols(wkr), jnp.zeros_like(wkr), wkr, jnp.zeros_like(wkr)],
                                axis=1).astype(BF16),
        "wg": w_in[:, o3:].astype(BF16), "gq": row(g_q), "gkv": row(g_kv),
        "w1": w1.astype(BF16), "w2": w2.astype(BF16), "wuk": wuk.astype(BF16), "wuvt": wuvt.astype(BF16),
        "wf": w_fourier.astype(BF16), "wa": w_attn.astype(BF16), "wo": w_o.astype(BF16),
        "wgate": w_gate.astype(BF16), "wup": w_up.astype(BF16), "wdown": w_down.astype(BF16),
        "ln1g": row(ln1_g), "ln1b": row(ln1_b), "ln2g": row(ln2_g), "ln2b": row(ln2_b),
    }


def _encoder_layer(x, w, alpha):
    b, s, _ = x.shape
    n1, n2 = DFT_N1, s // DFT_N1
    m1, t = _dft_tables(s)
    ab, qt, k, vt, gates = _proj_call(x, w, _rope_tables(s))
    y = _dft1_call(ab.reshape(b, 2, n1, n2, FOURIER_WIDTH), m1)
    x3 = _dft2_call(y, t)
    ot = _attn_call(qt, k, vt)
    return _final_call(x, gates, x3.reshape(b, s, FOURIER_WIDTH), ot, w, alpha)


def kernel(x_prompt, x_sample, w_in, w_fourier, g_q, w_uq, g_kv, w_ukv, w_attn, w_o, ln1_g, ln1_b, w_gate, w_up, w_down, ln2_g, ln2_b):
    stacked = (w_in, w_fourier, g_q, w_uq, g_kv, w_ukv, w_attn, w_o, ln1_g, ln1_b, w_gate, w_up, w_down, ln2_g, ln2_b)
    depth = w_in.shape[0]
    alpha = (2.0 * depth) ** 0.25
    layer = lambda t, l: t.reshape(t.shape[1:]) if depth == 1 else t[l]
    layers = [_layer_weights(*(layer(t, l) for t in stacked)) for l in range(depth)]
    outs = []
    for x in (x_prompt, x_sample):
        for w in layers:
            x = _encoder_layer(x, w, alpha)
        outs.append(x)
    return tuple(outs)
```

```python
import functools
import math

import jax
import jax.numpy as jnp
import numpy as np
from jax import lax
from jax.experimental import pallas as pl
from jax.experimental.pallas import tpu as pltpu

F32 = jnp.float32
BF16 = jnp.bfloat16

D_MODEL = 1024
N_GROUPS = 4
GROUP_DIM = 128
FOURIER_WIDTH = N_GROUPS * GROUP_DIM
N_HEADS = 8
NOPE_DIM = 64
ROPE_DIM = 32
QK_DIM = NOPE_DIM + ROPE_DIM
V_DIM = 64
ATTN_WIDTH = N_HEADS * V_DIM
Q_RANK = 384
KV_RANK = 256
ROPE_THETA = 10000.0
D_FF = 2816
LN_EPS = 1e-5
RMS_EPS = 1e-6

VMEM_BYTES_V7X = 64 * 1024 * 1024
LANES = 128

QK_PAD = 128
V_PAD = 80
DFT_N1 = 128
TOK_TILE = 512
DFT1_ROWS = 16
DFT2_ROWS = 2048
ATTN_TQ = 1024
ATTN_TK = 512
ATTN_UNROLL = 32
ATTN_INLINE_TILES = 4
ATTN_BIAS_ROW = QK_DIM
ATTN_BOUND_SLACK = 1.0 + 2.0 ** -6
ATTN_DENOM_MIN = 2.0 ** -60
ROW_GROUPS = 2


def _params(semantics, vmem_mb):
    return pltpu.CompilerParams(dimension_semantics=semantics, vmem_limit_bytes=vmem_mb * 1024 * 1024)


def _const_spec(shape):
    zeros = (0,) * len(shape)
    return pl.BlockSpec(shape, lambda *_: zeros, pipeline_mode=pl.Buffered(1))


def _dot(a, b):
    return jnp.dot(a, b, preferred_element_type=F32)


def _dot_nt(a, b):
    return lax.dot_general(a, b, (((1,), (1,)), ((), ())), preferred_element_type=F32)


def _dot_tn(a, b):
    return lax.dot_general(a, b, (((0,), (0,)), ((), ())), preferred_element_type=F32)


def _fold_kernel(wf_ref, cs_ref, out_ref):
    for g in range(N_GROUPS):
        w = wf_ref[:, g * GROUP_DIM:(g + 1) * GROUP_DIM]
        r = jnp.dot(w, cs_ref[...], preferred_element_type=F32, precision=lax.Precision.HIGHEST)
        out_ref[:, g * GROUP_DIM:(g + 1) * GROUP_DIM] = r[:, :GROUP_DIM].astype(BF16)
        out_ref[:, FOURIER_WIDTH + g * GROUP_DIM:FOURIER_WIDTH + (g + 1) * GROUP_DIM] = r[:, GROUP_DIM:].astype(BF16)


def _fold_call(w_f, cs):
    return pl.pallas_call(
        _fold_kernel,
        out_shape=jax.ShapeDtypeStruct((D_MODEL, 2 * FOURIER_WIDTH), BF16),
        name="fold_channel_dft",
    )(w_f, cs)


def _proj_kernel(x_ref, wab_ref, wlat_ref, wg_ref, gq_ref, gkv_ref,
                 w1_ref, w2_ref, wuk_ref, wuvt_ref, cosq_ref, sinq_ref, cosk_ref, sink_ref,
                 ab_ref, qt_ref, k_ref, vt_ref, g_ref):
    xb = x_ref[0].astype(BF16)
    o1 = Q_RANK
    o2 = o1 + KV_RANK
    lat = _dot(xb, wlat_ref[...])
    ab = _dot(xb, wab_ref[...])
    ab_ref[0, 0] = ab[:, :FOURIER_WIDTH].astype(BF16)
    ab_ref[0, 1] = ab[:, FOURIER_WIDTH:].astype(BF16)
    g_ref[0] = jax.nn.sigmoid(_dot(xb, wg_ref[...])).astype(BF16)

    cq = lat[:, :o1]
    cqn = (cq * lax.rsqrt(jnp.mean(jnp.square(cq), axis=-1, keepdims=True) + RMS_EPS) * gq_ref[...]).astype(BF16)
    ckv = lat[:, o1:o2]
    ckvn = (ckv * lax.rsqrt(jnp.mean(jnp.square(ckv), axis=-1, keepdims=True) + RMS_EPS) * gkv_ref[...]).astype(BF16)

    kraw = lat[:, o2:]
    kpe = kraw * cosk_ref[...] + pltpu.roll(kraw, QK_PAD // 2, axis=1) * sink_ref[...]
    lane = lax.broadcasted_iota(jnp.int32, (1, QK_PAD), 1)
    kpe = kpe + jnp.where(lane == ATTN_BIAS_ROW, 1.0, 0.0)
    kn = _dot(ckvn, wuk_ref[...])
    for h in range(N_HEADS):
        k_ref[0, h] = (kn[:, h * QK_PAD:(h + 1) * QK_PAD] + kpe).astype(BF16)

    qt = _dot_nt(w1_ref[...], cqn)
    qr = _dot_nt(w2_ref[...], cqn)
    cos = cosq_ref[...]
    sin = sinq_ref[...]
    for h in range(N_HEADS):
        r0 = h * QK_DIM
        qt_ref[0, h, 0:NOPE_DIM, :] = qt[r0:r0 + NOPE_DIM].astype(BF16)
        qt_ref[0, h, NOPE_DIM:QK_DIM, :] = (
            qt[r0 + NOPE_DIM:r0 + QK_DIM] * cos + qr[h * ROPE_DIM:(h + 1) * ROPE_DIM] * sin).astype(BF16)
        qt_ref[0, h, QK_DIM:QK_PAD, :] = jnp.zeros((QK_PAD - QK_DIM, qt.shape[1]), BF16)

    vt = _dot_nt(wuvt_ref[...], ckvn)
    for h in range(N_HEADS):
        vt_ref[0, h, 0:V_DIM, :] = vt[h * V_DIM:(h + 1) * V_DIM].astype(BF16)
        vt_ref[0, h, V_DIM:V_PAD, :] = jnp.ones((V_PAD - V_DIM, vt.shape[1]), BF16)


def _proj_call(x, w, tabs):
    b, s, _ = x.shape
    tm = TOK_TILE
    consts = [w["wab"], w["wlat"], w["wg"], w["gq"], w["gkv"], w["w1"], w["w2"], w["wuk"], w["wuvt"]]
    in_specs = [pl.BlockSpec((1, tm, D_MODEL), lambda bi, i: (bi, i, 0))]
    in_specs += [_const_spec(c.shape) for c in consts]
    in_specs += [
        pl.BlockSpec((ROPE_DIM, tm), lambda bi, i: (0, i)),
        pl.BlockSpec((ROPE_DIM, tm), lambda bi, i: (0, i)),
        pl.BlockSpec((tm, QK_PAD), lambda bi, i: (i, 0)),
        pl.BlockSpec((tm, QK_PAD), lambda bi, i: (i, 0)),
    ]
    out_shape = (
        jax.ShapeDtypeStruct((b, 2, s, FOURIER_WIDTH), BF16),
        jax.ShapeDtypeStruct((b, N_HEADS, QK_PAD, s), BF16),
        jax.ShapeDtypeStruct((b, N_HEADS, s, QK_PAD), BF16),
        jax.ShapeDtypeStruct((b, N_HEADS, V_PAD, s), BF16),
        jax.ShapeDtypeStruct((b, s, 2 * D_MODEL), BF16),
    )
    out_specs = (
        pl.BlockSpec((1, 2, tm, FOURIER_WIDTH), lambda bi, i: (bi, 0, i, 0)),
        pl.BlockSpec((1, N_HEADS, QK_PAD, tm), lambda bi, i: (bi, 0, 0, i)),
        pl.BlockSpec((1, N_HEADS, tm, QK_PAD), lambda bi, i: (bi, 0, i, 0)),
        pl.BlockSpec((1, N_HEADS, V_PAD, tm), lambda bi, i: (bi, 0, 0, i)),
        pl.BlockSpec((1, tm, 2 * D_MODEL), lambda bi, i: (bi, i, 0)),
    )
    return pl.pallas_call(
        _proj_kernel,
        grid=(b, s // tm),
        in_specs=in_specs,
        out_specs=out_specs,
        out_shape=out_shape,
        compiler_params=_params(("parallel", "parallel"), 56),
        name="proj",
    )(x, *consts, tabs["cosq"], tabs["sinq"], tabs["cosk"], tabs["sink"])


def _dft1_kernel(m1_ref, ab_ref, y_ref):
    _, _, n1, r, fw = ab_ref.shape
    zt = jnp.swapaxes(ab_ref[0].reshape(2 * n1, r, fw), 0, 1)
    m1 = m1_ref[...].astype(BF16)
    ys = [_dot(m1, zt[i]).astype(BF16) for i in range(r)]
    y_ref[0] = jnp.swapaxes(jnp.stack(ys, axis=0), 0, 1).reshape(2, n1, r, fw)


def _dft1_call(ab, m1):
    b, _, n1, n2, fw = ab.shape
    r = DFT1_ROWS
    return pl.pallas_call(
        _dft1_kernel,
        grid=(b, n2 // r),
        in_specs=[_const_spec(m1.shape), pl.BlockSpec((1, 2, n1, r, fw), lambda bi, i: (bi, 0, 0, i, 0))],
        out_specs=pl.BlockSpec((1, 2, n1, r, fw), lambda bi, i: (bi, 0, 0, i, 0)),
        out_shape=jax.ShapeDtypeStruct(ab.shape, BF16),
        compiler_params=_params(("parallel", "parallel"), 32),
        name="dft_stage1",
    )(m1, ab)


def _dft2_kernel(t_ref, y_ref, x_ref):
    xs = []
    for g in range(t_ref.shape[0]):
        yk = jnp.concatenate([y_ref[0, 0, g], y_ref[0, 1, g]], axis=0)
        xs.append(_dot(t_ref[g].astype(BF16), yk).astype(BF16))
    x_ref[0] = jnp.swapaxes(jnp.stack(xs, axis=0), 0, 1)


def _dft2_call(y, t):
    b, _, n1, n2, fw = y.shape
    g = min(n1, DFT2_ROWS // n2)
    return pl.pallas_call(
        _dft2_kernel,
        grid=(b, n1 // g),
        in_specs=[pl.BlockSpec((g, n2, 2 * n2), lambda bi, i: (i, 0, 0)),
                  pl.BlockSpec((1, 2, g, n2, fw), lambda bi, i: (bi, 0, i, 0, 0))],
        out_specs=pl.BlockSpec((1, n2, g, fw), lambda bi, i: (bi, 0, i, 0)),
        out_shape=jax.ShapeDtypeStruct((b, n2, n1, fw), BF16),
        compiler_params=_params(("parallel", "parallel"), 32),
        name="dft_stage2",
    )(t, y)


def _attn_kernel(qt_ref, k_ref, vt_ref, o_ref, qa_ref, m_ref, acc_ref):
    s_len = k_ref.shape[2]
    tq, tk = qa_ref.shape[1], ATTN_TK
    nblk = s_len // tk
    unroll = min(ATTN_UNROLL, nblk)

    def kv_block(j):
        koff = j * tk if isinstance(j, int) else pl.multiple_of(j * tk, tk)
        return k_ref[0, 0, pl.ds(koff, tk), :], vt_ref[0, 0, :, pl.ds(koff, tk)]

    ones = jnp.ones((16, QK_PAD), BF16)

    def key_norm(i, mx):
        for u in range(unroll):
            kf = kv_block(i * unroll + u)[0].astype(F32)
            mx = jnp.maximum(mx, _dot_nt(ones, (kf * kf).astype(BF16)))
        return mx

    ksq = jnp.max(lax.fori_loop(0, nblk // unroll, key_norm, jnp.zeros((16, tk), F32)))
    row = lax.broadcasted_iota(jnp.int32, (QK_PAD, tq), 0)

    n_qt = s_len // tq
    trips_per_qt = nblk // unroll

    def tile_offset(qi):
        return qi * tq if isinstance(qi, int) else pl.multiple_of(qi * tq, tq)

    def write_tile(qi, acc):
        o_ref[0, :, pl.ds(tile_offset(qi), tq)] = (acc[:V_DIM] / acc[V_DIM:V_DIM + 1]).astype(BF16)

    def needs_exact(acc):
        return jnp.logical_not(jnp.min(acc[V_DIM:V_DIM + 1]) >= ATTN_DENOM_MIN)

    def fast_tile(qi):
        qf = qt_ref[0, 0, :, pl.ds(tile_offset(qi), tq)].astype(F32)
        bound = jnp.sqrt(jnp.sum(qf * qf, axis=0, keepdims=True) * ksq) * ATTN_BOUND_SLACK
        qa_ref[...] = jnp.where(row == ATTN_BIAS_ROW, -bound, qf).astype(BF16)

        def trip(t):
            part = None
            sc = _dot(kv_block(t * unroll)[0], qa_ref[...])
            for u in range(unroll):
                sc_next = _dot(kv_block(t * unroll + u + 1)[0], qa_ref[...]) if u + 1 < unroll else None
                pv = _dot(kv_block(t * unroll + u)[1], jnp.exp2(sc).astype(BF16))
                part = pv if part is None else part + pv
                sc = sc_next
            return part

        if trips_per_qt == 1:
            return trip(0)
        acc_ref[...] = jnp.zeros(acc_ref.shape, F32)

        def add_trip(t, c):
            acc_ref[...] += trip(t)
            return c

        lax.fori_loop(0, trips_per_qt, add_trip, 0)
        return acc_ref[...]

    def exact_tile(qi):
        qt = qt_ref[0, 0, :, pl.ds(tile_offset(qi), tq)]
        m_ref[...] = jnp.full(m_ref.shape, -1e30, F32)
        acc_ref[...] = jnp.zeros(acc_ref.shape, F32)

        def exact_block(j, c):
            kb, vb = kv_block(j)
            sc = _dot(kb, qt)
            m_old = m_ref[...]
            m_new = jnp.maximum(m_old, jnp.max(sc, axis=0, keepdims=True))
            acc_ref[...] = acc_ref[...] * jnp.exp2(m_old - m_new) + _dot(vb, jnp.exp2(sc - m_new).astype(BF16))
            m_ref[...] = m_new
            return c

        lax.fori_loop(0, nblk, exact_block, 0)
        write_tile(qi, acc_ref[...])

    if trips_per_qt == 1 and n_qt <= ATTN_INLINE_TILES:
        redo = []
        for qi in range(n_qt):
            acc = fast_tile(qi)
            write_tile(qi, acc)
            redo.append(needs_exact(acc))
        for qi in range(n_qt):
            pl.when(redo[qi])(functools.partial(exact_tile, qi))
    else:
        def q_tile(qi, carry):
            acc = fast_tile(qi)
            write_tile(qi, acc)
            pl.when(needs_exact(acc))(functools.partial(exact_tile, qi))
            return carry

        lax.fori_loop(0, n_qt, q_tile, 0)


def _attn_call(qt, k, vt):
    b, h, _, s = qt.shape
    tq = ATTN_TQ
    return pl.pallas_call(
        _attn_kernel,
        grid=(b, h),
        in_specs=[pl.BlockSpec((1, 1, QK_PAD, s), lambda bi, hi: (bi, hi, 0, 0)),
                  pl.BlockSpec((1, 1, s, QK_PAD), lambda bi, hi: (bi, hi, 0, 0)),
                  pl.BlockSpec((1, 1, V_PAD, s), lambda bi, hi: (bi, hi, 0, 0))],
        out_specs=pl.BlockSpec((1, V_DIM, s), lambda bi, hi: (bi, hi, 0)),
        out_shape=jax.ShapeDtypeStruct((b, ATTN_WIDTH, s), BF16),
        scratch_shapes=[pltpu.VMEM((QK_PAD, tq), BF16), pltpu.VMEM((1, tq), F32), pltpu.VMEM((V_PAD, tq), F32)],
        compiler_params=_params(("parallel", "parallel"), 48),
        name="attention",
    )(qt, k, vt)


def _layer_norm(v, g, b):
    mu = jnp.mean(v, axis=-1, keepdims=True)
    var = jnp.mean(jnp.square(v - mu), axis=-1, keepdims=True)
    return (v - mu) * lax.rsqrt(var + LN_EPS) * g + b


def _final_kernel(alpha, x_ref, g_ref, f_ref, ot_ref, wf_ref, wa_ref, wo_ref, wgate_ref, wup_ref, wdown_ref,
                  ln1g_ref, ln1b_ref, ln2g_ref, ln2b_ref, y_ref):
    rows = x_ref.shape[1] // ROW_GROUPS
    groups = [slice(r * rows, (r + 1) * rows) for r in range(ROW_GROUPS)]
    mix = []
    for rs in groups:
        branch_a = _dot(f_ref[0, rs], wf_ref[...])
        branch_b = _dot_tn(ot_ref[0, :, rs], wa_ref[...])
        g = g_ref[0, rs]
        merged = (g[:, :D_MODEL].astype(F32) * branch_a + g[:, D_MODEL:].astype(F32) * branch_b).astype(BF16)
        mix.append(_dot(merged, wo_ref[...]))
    hs = [_layer_norm(alpha * x_ref[0, rs] + m, ln1g_ref[...], ln1b_ref[...]) for rs, m in zip(groups, mix)]
    acts = []
    for h in hs:
        hb = h.astype(BF16)
        acts.append((jax.nn.silu(_dot(hb, wgate_ref[...])) * _dot(hb, wup_ref[...])).astype(BF16))
    ffns = [_dot(a, wdown_ref[...]) for a in acts]
    for rs, h, ffn in zip(groups, hs, ffns):
        y_ref[0, rs] = _layer_norm(alpha * h + ffn, ln2g_ref[...], ln2b_ref[...])


def _final_call(x, gates, x3, ot, w, alpha):
    b, s, _ = x.shape
    tm = TOK_TILE
    consts = [w["wf"], w["wa"], w["wo"], w["wgate"], w["wup"], w["wdown"],
              w["ln1g"], w["ln1b"], w["ln2g"], w["ln2b"]]
    in_specs = [pl.BlockSpec((1, tm, D_MODEL), lambda bi, i: (bi, i, 0)),
                pl.BlockSpec((1, tm, 2 * D_MODEL), lambda bi, i: (bi, i, 0)),
                pl.BlockSpec((1, tm, FOURIER_WIDTH), lambda bi, i: (bi, i, 0)),
                pl.BlockSpec((1, ATTN_WIDTH, tm), lambda bi, i: (bi, 0, i))]
    in_specs += [_const_spec(c.shape) for c in consts]
    return pl.pallas_call(
        functools.partial(_final_kernel, alpha),
        grid=(b, s // tm),
        in_specs=in_specs,
        out_specs=pl.BlockSpec((1, tm, D_MODEL), lambda bi, i: (bi, i, 0)),
        out_shape=jax.ShapeDtypeStruct((b, s, D_MODEL), F32),
        compiler_params=_params(("parallel", "parallel"), 56),
        name="merge_ffn",
    )(x, gates, x3, ot, *consts)


def _cos_sin(num, den):
    ang = (2.0 * np.pi / den) * (num % den).astype(np.float64)
    return np.cos(ang), np.sin(ang)


def _channel_dft_table():
    j = np.arange(GROUP_DIM)
    c, s = _cos_sin(j[:, None] * j[None, :], GROUP_DIM)
    return (np.concatenate([c, s], axis=1) * GROUP_DIM ** -0.5).astype(np.float32)


def _dft_tables(s_len):
    n1, n2 = DFT_N1, s_len // DFT_N1
    i1 = np.arange(n1)
    c, s = _cos_sin(i1[:, None] * i1[None, :], n1)
    m1 = np.block([[c, -s], [-s, -c]]) * n1 ** -0.5
    i2 = np.arange(n2)
    k = i1[:, None, None] + n1 * i2[None, :, None]
    tc, ts = _cos_sin(k * i2[None, None, :], s_len)
    t = np.concatenate([tc, ts], axis=2) * n2 ** -0.5
    return m1.astype(np.float32), t.astype(np.float32)


def _rope_tables(s_len):
    f32 = np.float32
    inv_freq = 1.0 / ROPE_THETA ** (np.arange(0, ROPE_DIM, 2, dtype=np.float64) / ROPE_DIM)
    ang = np.arange(s_len, dtype=np.float64)[:, None] * inv_freq[None, :]
    cos2 = np.concatenate([np.cos(ang)] * 2, axis=1).astype(f32)
    sin2 = np.concatenate([np.sin(ang)] * 2, axis=1).astype(f32)
    pad = lambda t: np.pad(t, ((0, 0), (NOPE_DIM, QK_PAD - QK_DIM)))
    return {"cosq": np.ascontiguousarray(cos2.T), "sinq": np.ascontiguousarray(sin2.T),
            "cosk": pad(cos2), "sink": pad(sin2)}


def _rotate_half_cols(w):
    half = ROPE_DIM // 2
    return jnp.concatenate([-w[..., half:], w[..., :half]], axis=-1)


def _layer_weights(w_in, w_fourier, g_q, w_uq, g_kv, w_ukv, w_attn, w_o,
                   ln1_g, ln1_b, w_gate, w_up, w_down, ln2_g, ln2_b):
    o0 = FOURIER_WIDTH
    o1 = o0 + Q_RANK
    o2 = o1 + KV_RANK
    o3 = o2 + ROPE_DIM
    wkr = w_in[:, o2:o3]
    q_scale = (QK_DIM ** -0.5) * math.log2(math.e)
    uq = (w_uq * q_scale).reshape(Q_RANK, N_HEADS, QK_DIM)
    w1 = uq.reshape(Q_RANK, N_HEADS * QK_DIM).T
    w2 = _rotate_half_cols(uq[:, :, NOPE_DIM:]).reshape(Q_RANK, N_HEADS * ROPE_DIM).T
    ukv = w_ukv.reshape(KV_RANK, N_HEADS, NOPE_DIM + V_DIM)
    wuk = jnp.pad(ukv[:, :, :NOPE_DIM], ((0, 0), (0, 0), (0, QK_PAD - NOPE_DIM))).reshape(KV_RANK, N_HEADS * QK_PAD)
    wuvt = ukv[:, :, NOPE_DIM:].reshape(KV_RANK, ATTN_WIDTH).T
    row = lambda v: v.reshape(1, -1).astype(F32)
    return {
        "wab": _fold_call(w_in[:, :o0], _channel_dft_table()),
        "wlat": jnp.concatenate([w_in[:, o0:o2], _rotate_half_cols(wkr), jnp.zeros_like(wkr), wkr, jnp.zeros_like(wkr)],
                                axis=1).astype(BF16),
        "wg": w_in[:, o3:].astype(BF16), "gq": row(g_q), "gkv": row(g_kv),
        "w1": w1.astype(BF16), "w2": w2.astype(BF16), "wuk": wuk.astype(BF16), "wuvt": wuvt.astype(BF16),
        "wf": w_fourier.astype(BF16), "wa": w_attn.astype(BF16), "wo": w_o.astype(BF16),
        "wgate": w_gate.astype(BF16), "wup": w_up.astype(BF16), "wdown": w_down.astype(BF16),
        "ln1g": row(ln1_g), "ln1b": row(ln1_b), "ln2g": row(ln2_g), "ln2b": row(ln2_b),
    }


def _encoder_layer(x, w, alpha):
    b, s, _ = x.shape
    n1, n2 = DFT_N1, s // DFT_N1
    m1, t = _dft_tables(s)
    ab, qt, k, vt, gates = _proj_call(x, w, _rope_tables(s))
    y = _dft1_call(ab.reshape(b, 2, n1, n2, FOURIER_WIDTH), m1)
    x3 = _dft2_call(y, t)
    ot = _attn_call(qt, k, vt)
    return _final_call(x, gates, x3.reshape(b, s, FOURIER_WIDTH), ot, w, alpha)


def kernel(x_prompt, x_sample, w_in, w_fourier, g_q, w_uq, g_kv, w_ukv, w_attn, w_o, ln1_g, ln1_b, w_gate, w_up, w_down, ln2_g, ln2_b):
    stacked = (w_in, w_fourier, g_q, w_uq, g_kv, w_ukv, w_attn, w_o, ln1_g, ln1_b, w_gate, w_up, w_down, ln2_g, ln2_b)
    depth = w_in.shape[0]
    alpha = (2.0 * depth) ** 0.25
    layer = lambda t, l: t.reshape(t.shape[1:]) if depth == 1 else t[l]
    layers = [_layer_weights(*(layer(t, l) for t in stacked)) for l in range(depth)]
    outs = []
    for x in (x_prompt, x_sample):
        for w in layers:
            x = _encoder_layer(x, w, alpha)
        outs.append(x)
    return tuple(outs)
```

```python
import functools
import math

import jax
import jax.numpy as jnp
import numpy as np
from jax import lax
from jax.experimental import pallas as pl
from jax.experimental.pallas import tpu as pltpu

F32 = jnp.float32
BF16 = jnp.bfloat16

D_MODEL = 1024
N_GROUPS = 4
GROUP_DIM = 128
FOURIER_WIDTH = N_GROUPS * GROUP_DIM
N_HEADS = 8
NOPE_DIM = 64
ROPE_DIM = 32
QK_DIM = NOPE_DIM + ROPE_DIM
V_DIM = 64
ATTN_WIDTH = N_HEADS * V_DIM
Q_RANK = 384
KV_RANK = 256
ROPE_THETA = 10000.0
D_FF = 2816
LN_EPS = 1e-5
RMS_EPS = 1e-6

VMEM_BYTES_V7X = 64 * 1024 * 1024
VMEM_WEIGHT_RESIDENT = VMEM_BYTES_V7X * 7 // 8
VMEM_HEAD_RESIDENT = VMEM_BYTES_V7X * 3 // 4
VMEM_STREAMING = VMEM_BYTES_V7X // 2

QK_PAD = 128
V_PAD = 80
DFT_N1 = 128
TOK_TILE = 512
DFT1_ROWS = 16
DFT2_ROWS = 2048
ATTN_TQ = 1024
ATTN_TK = 512
ATTN_UNROLL = 32
ATTN_BIAS_ROW = QK_DIM
ATTN_BOUND_SLACK = 1.0 + 2.0 ** -6
ATTN_DENOM_MIN = 2.0 ** -60
ROW_GROUPS = 2


def _params(semantics, vmem_bytes):
    return pltpu.CompilerParams(dimension_semantics=semantics, vmem_limit_bytes=vmem_bytes)


def _const_spec(shape):
    zeros = (0,) * len(shape)
    return pl.BlockSpec(shape, lambda *_: zeros, pipeline_mode=pl.Buffered(1))


def _dot(a, b):
    return jnp.dot(a, b, preferred_element_type=F32)


def _dot_nt(a, b):
    return lax.dot_general(a, b, (((1,), (1,)), ((), ())), preferred_element_type=F32)


def _dot_tn(a, b):
    return lax.dot_general(a, b, (((0,), (0,)), ((), ())), preferred_element_type=F32)


def _fold_kernel(wf_ref, cs_ref, out_ref):
    for g in range(N_GROUPS):
        w = wf_ref[:, g * GROUP_DIM:(g + 1) * GROUP_DIM]
        r = jnp.dot(w, cs_ref[...], preferred_element_type=F32, precision=lax.Precision.HIGHEST)
        out_ref[:, g * GROUP_DIM:(g + 1) * GROUP_DIM] = r[:, :GROUP_DIM].astype(BF16)
        out_ref[:, FOURIER_WIDTH + g * GROUP_DIM:FOURIER_WIDTH + (g + 1) * GROUP_DIM] = r[:, GROUP_DIM:].astype(BF16)


def _fold_call(w_f, cs):
    return pl.pallas_call(
        _fold_kernel,
        out_shape=jax.ShapeDtypeStruct((D_MODEL, 2 * FOURIER_WIDTH), BF16),
        name="fold_channel_dft",
    )(w_f, cs)


def _proj_kernel(x_ref, wab_ref, wlat_ref, wg_ref, gq_ref, gkv_ref,
                 w1_ref, w2_ref, wuk_ref, wuvt_ref, cosq_ref, sinq_ref, cosk_ref, sink_ref,
                 ab_ref, qt_ref, k_ref, vt_ref, g_ref):
    xb = x_ref[0].astype(BF16)
    o1 = Q_RANK
    o2 = o1 + KV_RANK
    lat = _dot(xb, wlat_ref[...])
    ab = _dot(xb, wab_ref[...])
    ab_ref[0, 0] = ab[:, :FOURIER_WIDTH].astype(BF16)
    ab_ref[0, 1] = ab[:, FOURIER_WIDTH:].astype(BF16)
    g_ref[0] = jax.nn.sigmoid(_dot(xb, wg_ref[...])).astype(BF16)

    cq = lat[:, :o1]
    cqn = (cq * lax.rsqrt(jnp.mean(jnp.square(cq), axis=-1, keepdims=True) + RMS_EPS) * gq_ref[...]).astype(BF16)
    ckv = lat[:, o1:o2]
    ckvn = (ckv * lax.rsqrt(jnp.mean(jnp.square(ckv), axis=-1, keepdims=True) + RMS_EPS) * gkv_ref[...]).astype(BF16)

    kraw = lat[:, o2:]
    kpe = kraw * cosk_ref[...] + pltpu.roll(kraw, QK_PAD // 2, axis=1) * sink_ref[...]
    lane = lax.broadcasted_iota(jnp.int32, (1, QK_PAD), 1)
    kpe = kpe + jnp.where(lane == ATTN_BIAS_ROW, 1.0, 0.0)
    kn = _dot(ckvn, wuk_ref[...])
    for h in range(N_HEADS):
        k_ref[0, h] = (kn[:, h * QK_PAD:(h + 1) * QK_PAD] + kpe).astype(BF16)

    qt = _dot_nt(w1_ref[...], cqn)
    qr = _dot_nt(w2_ref[...], cqn)
    cos = cosq_ref[...]
    sin = sinq_ref[...]
    for h in range(N_HEADS):
        r0 = h * QK_DIM
        qt_ref[0, h, 0:NOPE_DIM, :] = qt[r0:r0 + NOPE_DIM].astype(BF16)
        qt_ref[0, h, NOPE_DIM:QK_DIM, :] = (
            qt[r0 + NOPE_DIM:r0 + QK_DIM] * cos + qr[h * ROPE_DIM:(h + 1) * ROPE_DIM] * sin).astype(BF16)
        qt_ref[0, h, QK_DIM:QK_PAD, :] = jnp.zeros((QK_PAD - QK_DIM, qt.shape[1]), BF16)

    vt = _dot_nt(wuvt_ref[...], ckvn)
    for h in range(N_HEADS):
        vt_ref[0, h, 0:V_DIM, :] = vt[h * V_DIM:(h + 1) * V_DIM].astype(BF16)
        vt_ref[0, h, V_DIM:V_PAD, :] = jnp.ones((V_PAD - V_DIM, vt.shape[1]), BF16)


def _proj_call(x, w, tabs):
    b, s, _ = x.shape
    tm = TOK_TILE
    consts = [w["wab"], w["wlat"], w["wg"], w["gq"], w["gkv"], w["w1"], w["w2"], w["wuk"], w["wuvt"]]
    in_specs = [pl.BlockSpec((1, tm, D_MODEL), lambda bi, i: (bi, i, 0))]
    in_specs += [_const_spec(c.shape) for c in consts]
    in_specs += [
        pl.BlockSpec((ROPE_DIM, tm), lambda bi, i: (0, i)),
        pl.BlockSpec((ROPE_DIM, tm), lambda bi, i: (0, i)),
        pl.BlockSpec((tm, QK_PAD), lambda bi, i: (i, 0)),
        pl.BlockSpec((tm, QK_PAD), lambda bi, i: (i, 0)),
    ]
    out_shape = (
        jax.ShapeDtypeStruct((b, 2, s, FOURIER_WIDTH), BF16),
        jax.ShapeDtypeStruct((b, N_HEADS, QK_PAD, s), BF16),
        jax.ShapeDtypeStruct((b, N_HEADS, s, QK_PAD), BF16),
        jax.ShapeDtypeStruct((b, N_HEADS, V_PAD, s), BF16),
        jax.ShapeDtypeStruct((b, s, 2 * D_MODEL), BF16),
    )
    out_specs = (
        pl.BlockSpec((1, 2, tm, FOURIER_WIDTH), lambda bi, i: (bi, 0, i, 0)),
        pl.BlockSpec((1, N_HEADS, QK_PAD, tm), lambda bi, i: (bi, 0, 0, i)),
        pl.BlockSpec((1, N_HEADS, tm, QK_PAD), lambda bi, i: (bi, 0, i, 0)),
        pl.BlockSpec((1, N_HEADS, V_PAD, tm), lambda bi, i: (bi, 0, 0, i)),
        pl.BlockSpec((1, tm, 2 * D_MODEL), lambda bi, i: (bi, i, 0)),
    )
    return pl.pallas_call(
        _proj_kernel,
        grid=(b, s // tm),
        in_specs=in_specs,
        out_specs=out_specs,
        out_shape=out_shape,
        compiler_params=_params(("parallel", "parallel"), VMEM_WEIGHT_RESIDENT),
        name="proj",
    )(x, *consts, tabs["cosq"], tabs["sinq"], tabs["cosk"], tabs["sink"])


def _dft1_kernel(m1_ref, ab_ref, y_ref):
    _, _, n1, r, fw = ab_ref.shape
    zt = jnp.swapaxes(ab_ref[0].reshape(2 * n1, r, fw), 0, 1)
    m1 = m1_ref[...].astype(BF16)
    ys = [_dot(m1, zt[i]).astype(BF16) for i in range(r)]
    y_ref[0] = jnp.swapaxes(jnp.stack(ys, axis=0), 0, 1).reshape(2, n1, r, fw)


def _dft1_call(ab, m1):
    b, _, n1, n2, fw = ab.shape
    r = DFT1_ROWS
    return pl.pallas_call(
        _dft1_kernel,
        grid=(b, n2 // r),
        in_specs=[_const_spec(m1.shape), pl.BlockSpec((1, 2, n1, r, fw), lambda bi, i: (bi, 0, 0, i, 0))],
        out_specs=pl.BlockSpec((1, 2, n1, r, fw), lambda bi, i: (bi, 0, 0, i, 0)),
        out_shape=jax.ShapeDtypeStruct(ab.shape, BF16),
        compiler_params=_params(("parallel", "parallel"), VMEM_STREAMING),
        name="dft_stage1",
    )(m1, ab)


def _dft2_kernel(t_ref, y_ref, x_ref):
    xs = []
    for g in range(t_ref.shape[0]):
        yk = jnp.concatenate([y_ref[0, 0, g], y_ref[0, 1, g]], axis=0)
        xs.append(_dot(t_ref[g].astype(BF16), yk).astype(BF16))
    x_ref[0] = jnp.swapaxes(jnp.stack(xs, axis=0), 0, 1)


def _dft2_call(y, t):
    b, _, n1, n2, fw = y.shape
    g = min(n1, DFT2_ROWS // n2)
    return pl.pallas_call(
        _dft2_kernel,
        grid=(b, n1 // g),
        in_specs=[pl.BlockSpec((g, n2, 2 * n2), lambda bi, i: (i, 0, 0)),
                  pl.BlockSpec((1, 2, g, n2, fw), lambda bi, i: (bi, 0, i, 0, 0))],
        out_specs=pl.BlockSpec((1, n2, g, fw), lambda bi, i: (bi, 0, i, 0)),
        out_shape=jax.ShapeDtypeStruct((b, n2, n1, fw), BF16),
        compiler_params=_params(("parallel", "parallel"), VMEM_STREAMING),
        name="dft_stage2",
    )(t, y)


def _attn_kernel(qt_ref, k_ref, vt_ref, o_ref, qa_ref, m_ref, acc_ref):
    s_len = k_ref.shape[2]
    tq, tk = qa_ref.shape[1], ATTN_TK
    nblk = s_len // tk
    unroll = min(ATTN_UNROLL, nblk)

    def kv_block(j):
        koff = j * tk if isinstance(j, int) else pl.multiple_of(j * tk, tk)
        return k_ref[0, 0, pl.ds(koff, tk), :], vt_ref[0, 0, :, pl.ds(koff, tk)]

    ones = jnp.ones((16, QK_PAD), BF16)

    def key_norm(i, mx):
        for u in range(unroll):
            kb = kv_block(i * unroll + u)[0]
            mx = jnp.maximum(mx, _dot_nt(ones, kb * kb))
        return mx

    ksq = jnp.max(lax.fori_loop(0, nblk // unroll, key_norm, jnp.zeros((16, tk), F32)))
    row = lax.broadcasted_iota(jnp.int32, (QK_PAD, tq), 0)

    n_qt = s_len // tq
    trips_per_qt = nblk // unroll

    def tile_offset(qi):
        return qi * tq if isinstance(qi, int) else pl.multiple_of(qi * tq, tq)

    def write_tile(qi, acc):
        o_ref[0, :, pl.ds(tile_offset(qi), tq)] = (acc[:V_DIM] / acc[V_DIM:V_DIM + 1]).astype(BF16)

    def needs_exact(acc):
        return jnp.logical_not(jnp.min(acc[V_DIM:V_DIM + 1]) >= ATTN_DENOM_MIN)

    def fast_tile(qi):
        qf = qt_ref[0, 0, :, pl.ds(tile_offset(qi), tq)].astype(F32)
        bound = jnp.sqrt(jnp.sum(qf * qf, axis=0, keepdims=True) * ksq) * ATTN_BOUND_SLACK
        qa_ref[...] = jnp.where(row == ATTN_BIAS_ROW, -bound, qf).astype(BF16)

        def trip(t):
            part = None
            sc = _dot(kv_block(t * unroll)[0], qa_ref[...])
            for u in range(unroll):
                sc_next = _dot(kv_block(t * unroll + u + 1)[0], qa_ref[...]) if u + 1 < unroll else None
                pv = _dot(kv_block(t * unroll + u)[1], jnp.exp2(sc).astype(BF16))
                part = pv if part is None else part + pv
                sc = sc_next
            return part

        if trips_per_qt == 1:
            return trip(0)
        acc_ref[...] = jnp.zeros(acc_ref.shape, F32)

        def add_trip(t, c):
            acc_ref[...] += trip(t)
            return c

        lax.fori_loop(0, trips_per_qt, add_trip, 0)
        return acc_ref[...]

    def exact_tile(qi):
        qt = qt_ref[0, 0, :, pl.ds(tile_offset(qi), tq)]
        m_ref[...] = jnp.full(m_ref.shape, -1e30, F32)
        acc_ref[...] = jnp.zeros(acc_ref.shape, F32)

        def exact_block(j, c):
            kb, vb = kv_block(j)
            sc = _dot(kb, qt)
            m_old = m_ref[...]
            m_new = jnp.maximum(m_old, jnp.max(sc, axis=0, keepdims=True))
            acc_ref[...] = acc_ref[...] * jnp.exp2(m_old - m_new) + _dot(vb, jnp.exp2(sc - m_new).astype(BF16))
            m_ref[...] = m_new
            return c

        lax.fori_loop(0, nblk, exact_block, 0)
        write_tile(qi, acc_ref[...])

    def q_tile(qi, carry):
        acc = fast_tile(qi)
        write_tile(qi, acc)
        pl.when(needs_exact(acc))(functools.partial(exact_tile, qi))
        return carry

    lax.fori_loop(0, n_qt, q_tile, 0)


def _attn_call(qt, k, vt):
    b, h, _, s = qt.shape
    tq = ATTN_TQ
    return pl.pallas_call(
        _attn_kernel,
        grid=(b, h),
        in_specs=[pl.BlockSpec((1, 1, QK_PAD, s), lambda bi, hi: (bi, hi, 0, 0)),
                  pl.BlockSpec((1, 1, s, QK_PAD), lambda bi, hi: (bi, hi, 0, 0)),
                  pl.BlockSpec((1, 1, V_PAD, s), lambda bi, hi: (bi, hi, 0, 0))],
        out_specs=pl.BlockSpec((1, V_DIM, s), lambda bi, hi: (bi, hi, 0)),
        out_shape=jax.ShapeDtypeStruct((b, ATTN_WIDTH, s), BF16),
        scratch_shapes=[pltpu.VMEM((QK_PAD, tq), BF16), pltpu.VMEM((1, tq), F32), pltpu.VMEM((V_PAD, tq), F32)],
        compiler_params=_params(("parallel", "parallel"), VMEM_HEAD_RESIDENT),
        name="attention",
    )(qt, k, vt)


def _layer_norm(v, g, b):
    mu = jnp.mean(v, axis=-1, keepdims=True)
    var = jnp.mean(jnp.square(v - mu), axis=-1, keepdims=True)
    return (v - mu) * lax.rsqrt(var + LN_EPS) * g + b


def _final_kernel(alpha, x_ref, g_ref, f_ref, ot_ref, wf_ref, wa_ref, wo_ref, wgate_ref, wup_ref, wdown_ref,
                  ln1g_ref, ln1b_ref, ln2g_ref, ln2b_ref, y_ref):
    rows = x_ref.shape[1] // ROW_GROUPS
    groups = [slice(r * rows, (r + 1) * rows) for r in range(ROW_GROUPS)]
    mix = []
    for rs in groups:
        branch_a = _dot(f_ref[0, rs], wf_ref[...])
        branch_b = _dot_tn(ot_ref[0, :, rs], wa_ref[...])
        g = g_ref[0, rs]
        merged = (g[:, :D_MODEL].astype(F32) * branch_a + g[:, D_MODEL:].astype(F32) * branch_b).astype(BF16)
        mix.append(_dot(merged, wo_ref[...]))
    hs = [_layer_norm(alpha * x_ref[0, rs] + m, ln1g_ref[...], ln1b_ref[...]) for rs, m in zip(groups, mix)]
    acts = []
    for h in hs:
        hb = h.astype(BF16)
        acts.append((jax.nn.silu(_dot(hb, wgate_ref[...])) * _dot(hb, wup_ref[...])).astype(BF16))
    ffns = [_dot(a, wdown_ref[...]) for a in acts]
    for rs, h, ffn in zip(groups, hs, ffns):
        y_ref[0, rs] = _layer_norm(alpha * h + ffn, ln2g_ref[...], ln2b_ref[...])


def _final_call(x, gates, x3, ot, w, alpha):
    b, s, _ = x.shape
    tm = TOK_TILE
    consts = [w["wf"], w["wa"], w["wo"], w["wgate"], w["wup"], w["wdown"],
              w["ln1g"], w["ln1b"], w["ln2g"], w["ln2b"]]
    in_specs = [pl.BlockSpec((1, tm, D_MODEL), lambda bi, i: (bi, i, 0)),
                pl.BlockSpec((1, tm, 2 * D_MODEL), lambda bi, i: (bi, i, 0)),
                pl.BlockSpec((1, tm, FOURIER_WIDTH), lambda bi, i: (bi, i, 0)),
                pl.BlockSpec((1, ATTN_WIDTH, tm), lambda bi, i: (bi, 0, i))]
    in_specs += [_const_spec(c.shape) for c in consts]
    return pl.pallas_call(
        functools.partial(_final_kernel, alpha),
        grid=(b, s // tm),
        in_specs=in_specs,
        out_specs=pl.BlockSpec((1, tm, D_MODEL), lambda bi, i: (bi, i, 0)),
        out_shape=jax.ShapeDtypeStruct((b, s, D_MODEL), F32),
        compiler_params=_params(("parallel", "parallel"), VMEM_WEIGHT_RESIDENT),
        name="merge_ffn",
    )(x, gates, x3, ot, *consts)


def _cos_sin(num, den):
    ang = (2.0 * np.pi / den) * (num % den).astype(np.float64)
    return np.cos(ang), np.sin(ang)


def _channel_dft_table():
    j = np.arange(GROUP_DIM)
    c, s = _cos_sin(j[:, None] * j[None, :], GROUP_DIM)
    return (np.concatenate([c, s], axis=1) * GROUP_DIM ** -0.5).astype(np.float32)


def _dft_tables(s_len):
    n1, n2 = DFT_N1, s_len // DFT_N1
    i1 = np.arange(n1)
    c, s = _cos_sin(i1[:, None] * i1[None, :], n1)
    m1 = np.block([[c, -s], [-s, -c]]) * n1 ** -0.5
    i2 = np.arange(n2)
    k = i1[:, None, None] + n1 * i2[None, :, None]
    tc, ts = _cos_sin(k * i2[None, None, :], s_len)
    t = np.concatenate([tc, ts], axis=2) * n2 ** -0.5
    return m1.astype(np.float32), t.astype(np.float32)


def _rope_tables(s_len):
    f32 = np.float32
    inv_freq = 1.0 / ROPE_THETA ** (np.arange(0, ROPE_DIM, 2, dtype=np.float64) / ROPE_DIM)
    ang = np.arange(s_len, dtype=np.float64)[:, None] * inv_freq[None, :]
    cos2 = np.concatenate([np.cos(ang)] * 2, axis=1).astype(f32)
    sin2 = np.concatenate([np.sin(ang)] * 2, axis=1).astype(f32)
    pad = lambda t: np.pad(t, ((0, 0), (NOPE_DIM, QK_PAD - QK_DIM)))
    return {"cosq": np.ascontiguousarray(cos2.T), "sinq": np.ascontiguousarray(sin2.T),
            "cosk": pad(cos2), "sink": pad(sin2)}


def _rotate_half_cols(w):
    half = ROPE_DIM // 2
    return jnp.concatenate([-w[..., half:], w[..., :half]], axis=-1)


def _layer_weights(w_in, w_fourier, g_q, w_uq, g_kv, w_ukv, w_attn, w_o,
                   ln1_g, ln1_b, w_gate, w_up, w_down, ln2_g, ln2_b):
    o0 = FOURIER_WIDTH
    o1 = o0 + Q_RANK
    o2 = o1 + KV_RANK
    o3 = o2 + ROPE_DIM
    wkr = w_in[:, o2:o3]
    q_scale = (QK_DIM ** -0.5) * math.log2(math.e)
    uq = (w_uq * q_scale).reshape(Q_RANK, N_HEADS, QK_DIM)
    w1 = uq.reshape(Q_RANK, N_HEADS * QK_DIM).T
    w2 = _rotate_half_cols(uq[:, :, NOPE_DIM:]).reshape(Q_RANK, N_HEADS * ROPE_DIM).T
    ukv = w_ukv.reshape(KV_RANK, N_HEADS, NOPE_DIM + V_DIM)
    wuk = jnp.pad(ukv[:, :, :NOPE_DIM], ((0, 0), (0, 0), (0, QK_PAD - NOPE_DIM))).reshape(KV_RANK, N_HEADS * QK_PAD)
    wuvt = ukv[:, :, NOPE_DIM:].reshape(KV_RANK, ATTN_WIDTH).T
    row = lambda v: v.reshape(1, -1).astype(F32)
    return {
        "wab": _fold_call(w_in[:, :o0], _channel_dft_table()),
        "wlat": jnp.concatenate([w_in[:, o0:o2], _rotate_half_cols(wkr), jnp.zeros_like(wkr), wkr, jnp.zeros_like(wkr)],
                                axis=1).astype(BF16),
        "wg": w_in[:, o3:].astype(BF16), "gq": row(g_q), "gkv": row(g_kv),
        "w1": w1.astype(BF16), "w2": w2.astype(BF16), "wuk": wuk.astype(BF16), "wuvt": wuvt.astype(BF16),
        "wf": w_fourier.astype(BF16), "wa": w_attn.astype(BF16), "wo": w_o.astype(BF16),
        "wgate": w_gate.astype(BF16), "wup": w_up.astype(BF16), "wdown": w_down.astype(BF16),
        "ln1g": row(ln1_g), "ln1b": row(ln1_b), "ln2g": row(ln2_g), "ln2b": row(ln2_b),
    }


def _encoder_layer(x, w, alpha):
    b, s, _ = x.shape
    n1, n2 = DFT_N1, s // DFT_N1
    m1, t = _dft_tables(s)
    ab, qt, k, vt, gates = _proj_call(x, w, _rope_tables(s))
    y = _dft1_call(ab.reshape(b, 2, n1, n2, FOURIER_WIDTH), m1)
    x3 = _dft2_call(y, t)
    ot = _attn_call(qt, k, vt)
    return _final_call(x, gates, x3.reshape(b, s, FOURIER_WIDTH), ot, w, alpha)


def kernel(x_prompt, x_sample, w_in, w_fourier, g_q, w_uq, g_kv, w_ukv, w_attn, w_o, ln1_g, ln1_b, w_gate, w_up, w_down, ln2_g, ln2_b):
    stacked = (w_in, w_fourier, g_q, w_uq, g_kv, w_ukv, w_attn, w_o, ln1_g, ln1_b, w_gate, w_up, w_down, ln2_g, ln2_b)
    depth = w_in.shape[0]
    alpha = (2.0 * depth) ** 0.25
    layer = lambda t, l: t.reshape(t.shape[1:]) if depth == 1 else t[l]
    layers = [_layer_weights(*(layer(t, l) for t in stacked)) for l in range(depth)]
    outs = []
    for x in (x_prompt, x_sample):
        for w in layers:
            x = _encoder_layer(x, w, alpha)
        outs.append(x)
    return tuple(outs)
```

```python
import functools
import math

import jax
import jax.numpy as jnp
import numpy as np
from jax import lax
from jax.experimental import pallas as pl
from jax.experimental.pallas import tpu as pltpu

F32 = jnp.float32
BF16 = jnp.bfloat16

D_MODEL = 1024
N_GROUPS = 4
GROUP_DIM = 128
FOURIER_WIDTH = N_GROUPS * GROUP_DIM
N_HEADS = 8
NOPE_DIM = 64
ROPE_DIM = 32
QK_DIM = NOPE_DIM + ROPE_DIM
V_DIM = 64
ATTN_WIDTH = N_HEADS * V_DIM
Q_RANK = 384
KV_RANK = 256
ROPE_THETA = 10000.0
D_FF = 2816
LN_EPS = 1e-5
RMS_EPS = 1e-6

VMEM_BYTES_V7X = 64 * 1024 * 1024
VMEM_WEIGHT_RESIDENT = VMEM_BYTES_V7X * 7 // 8
VMEM_HEAD_RESIDENT = VMEM_BYTES_V7X * 3 // 4
VMEM_STREAMING = VMEM_BYTES_V7X // 2

QK_PAD = 128
V_PAD = 80
DFT_N1 = 128
TOK_TILE = 512
DFT1_ROWS = 16
DFT2_ROWS = 2048
ATTN_TQ = 1024
ATTN_TK = 512
ATTN_UNROLL = 32
ATTN_BIAS_ROW = QK_DIM
ATTN_BOUND_SLACK = 1.0 + 2.0 ** -6
ATTN_DENOM_MIN = 2.0 ** -60
ROW_GROUPS = 2


def _params(semantics, vmem_bytes):
    return pltpu.CompilerParams(dimension_semantics=semantics, vmem_limit_bytes=vmem_bytes)


def _const_spec(shape):
    zeros = (0,) * len(shape)
    return pl.BlockSpec(shape, lambda *_: zeros, pipeline_mode=pl.Buffered(1))


def _dot(a, b):
    return jnp.dot(a, b, preferred_element_type=F32)


def _dot_nt(a, b):
    return lax.dot_general(a, b, (((1,), (1,)), ((), ())), preferred_element_type=F32)


def _dot_tn(a, b):
    return lax.dot_general(a, b, (((0,), (0,)), ((), ())), preferred_element_type=F32)


def _fold_kernel(wf_ref, cs_ref, out_ref):
    for g in range(N_GROUPS):
        w = wf_ref[:, g * GROUP_DIM:(g + 1) * GROUP_DIM]
        r = jnp.dot(w, cs_ref[...], preferred_element_type=F32, precision=lax.Precision.HIGHEST)
        out_ref[:, g * GROUP_DIM:(g + 1) * GROUP_DIM] = r[:, :GROUP_DIM].astype(BF16)
        out_ref[:, FOURIER_WIDTH + g * GROUP_DIM:FOURIER_WIDTH + (g + 1) * GROUP_DIM] = r[:, GROUP_DIM:].astype(BF16)


def _fold_call(w_f, cs):
    return pl.pallas_call(
        _fold_kernel,
        out_shape=jax.ShapeDtypeStruct((D_MODEL, 2 * FOURIER_WIDTH), BF16),
        name="fold_channel_dft",
    )(w_f, cs)


def _proj_kernel(x_ref, wab_ref, wlat_ref, wg_ref, gq_ref, gkv_ref,
                 w1_ref, wuk_ref, wuvt_ref, cosq_ref, sinq_ref, cosk_ref, sink_ref,
                 ab_ref, qt_ref, k_ref, vt_ref, g_ref):
    xb = x_ref[0].astype(BF16)
    o1 = Q_RANK
    o2 = o1 + KV_RANK
    lat = _dot(xb, wlat_ref[...])
    ab = _dot(xb, wab_ref[...])
    ab_ref[0, 0] = ab[:, :FOURIER_WIDTH].astype(BF16)
    ab_ref[0, 1] = ab[:, FOURIER_WIDTH:].astype(BF16)
    g_ref[0] = jax.nn.sigmoid(_dot(xb, wg_ref[...])).astype(BF16)

    cq = lat[:, :o1]
    cqn = (cq * lax.rsqrt(jnp.mean(jnp.square(cq), axis=-1, keepdims=True) + RMS_EPS) * gq_ref[...]).astype(BF16)
    ckv = lat[:, o1:o2]
    ckvn = (ckv * lax.rsqrt(jnp.mean(jnp.square(ckv), axis=-1, keepdims=True) + RMS_EPS) * gkv_ref[...]).astype(BF16)

    kraw = lat[:, o2:]
    kpe = kraw * cosk_ref[...] + pltpu.roll(kraw, QK_PAD // 2, axis=1) * sink_ref[...]
    lane = lax.broadcasted_iota(jnp.int32, (1, QK_PAD), 1)
    kpe = kpe + jnp.where(lane == ATTN_BIAS_ROW, 1.0, 0.0)
    kn = _dot(ckvn, wuk_ref[...])
    for h in range(N_HEADS):
        k_ref[0, h] = (kn[:, h * QK_PAD:(h + 1) * QK_PAD] + kpe).astype(BF16)

    qt = _dot_nt(w1_ref[...], cqn)
    cos = cosq_ref[...]
    sin = sinq_ref[...]
    half = ROPE_DIM // 2
    for h in range(N_HEADS):
        r0 = h * QK_DIM
        r1 = r0 + NOPE_DIM
        qt_ref[0, h, 0:NOPE_DIM, :] = qt[r0:r1].astype(BF16)
        rot = jnp.concatenate([-qt[r1 + half:r1 + ROPE_DIM], qt[r1:r1 + half]], axis=0)
        qt_ref[0, h, NOPE_DIM:QK_DIM, :] = (qt[r1:r1 + ROPE_DIM] * cos + rot * sin).astype(BF16)
        qt_ref[0, h, QK_DIM:QK_PAD, :] = jnp.zeros((QK_PAD - QK_DIM, qt.shape[1]), BF16)

    vt = _dot_nt(wuvt_ref[...], ckvn)
    for h in range(N_HEADS):
        vt_ref[0, h, 0:V_DIM, :] = vt[h * V_DIM:(h + 1) * V_DIM].astype(BF16)
        vt_ref[0, h, V_DIM:V_PAD, :] = jnp.ones((V_PAD - V_DIM, vt.shape[1]), BF16)


def _proj_call(x, w, tabs):
    b, s, _ = x.shape
    tm = TOK_TILE
    consts = [w["wab"], w["wlat"], w["wg"], w["gq"], w["gkv"], w["w1"], w["wuk"], w["wuvt"]]
    in_specs = [pl.BlockSpec((1, tm, D_MODEL), lambda bi, i: (bi, i, 0))]
    in_specs += [_const_spec(c.shape) for c in consts]
    in_specs += [
        pl.BlockSpec((ROPE_DIM, tm), lambda bi, i: (0, i)),
        pl.BlockSpec((ROPE_DIM, tm), lambda bi, i: (0, i)),
        pl.BlockSpec((tm, QK_PAD), lambda bi, i: (i, 0)),
        pl.BlockSpec((tm, QK_PAD), lambda bi, i: (i, 0)),
    ]
    out_shape = (
        jax.ShapeDtypeStruct((b, 2, s, FOURIER_WIDTH), BF16),
        jax.ShapeDtypeStruct((b, N_HEADS, QK_PAD, s), BF16),
        jax.ShapeDtypeStruct((b, N_HEADS, s, QK_PAD), BF16),
        jax.ShapeDtypeStruct((b, N_HEADS, V_PAD, s), BF16),
        jax.ShapeDtypeStruct((b, s, 2 * D_MODEL), BF16),
    )
    out_specs = (
        pl.BlockSpec((1, 2, tm, FOURIER_WIDTH), lambda bi, i: (bi, 0, i, 0)),
        pl.BlockSpec((1, N_HEADS, QK_PAD, tm), lambda bi, i: (bi, 0, 0, i)),
        pl.BlockSpec((1, N_HEADS, tm, QK_PAD), lambda bi, i: (bi, 0, i, 0)),
        pl.BlockSpec((1, N_HEADS, V_PAD, tm), lambda bi, i: (bi, 0, 0, i)),
        pl.BlockSpec((1, tm, 2 * D_MODEL), lambda bi, i: (bi, i, 0)),
    )
    return pl.pallas_call(
        _proj_kernel,
        grid=(b, s // tm),
        in_specs=in_specs,
        out_specs=out_specs,
        out_shape=out_shape,
        compiler_params=_params(("parallel", "parallel"), VMEM_WEIGHT_RESIDENT),
        name="proj",
    )(x, *consts, tabs["cosq"], tabs["sinq"], tabs["cosk"], tabs["sink"])


def _dft1_kernel(m1_ref, ab_ref, y_ref):
    _, _, n1, r, fw = ab_ref.shape
    zt = jnp.swapaxes(ab_ref[0].reshape(2 * n1, r, fw), 0, 1)
    m1 = m1_ref[...].astype(BF16)
    ys = [_dot(m1, zt[i]).astype(BF16) for i in range(r)]
    y_ref[0] = jnp.swapaxes(jnp.stack(ys, axis=0), 0, 1).reshape(2, n1, r, fw)


def _dft1_call(ab, m1):
    b, _, n1, n2, fw = ab.shape
    r = DFT1_ROWS
    return pl.pallas_call(
        _dft1_kernel,
        grid=(b, n2 // r),
        in_specs=[_const_spec(m1.shape), pl.BlockSpec((1, 2, n1, r, fw), lambda bi, i: (bi, 0, 0, i, 0))],
        out_specs=pl.BlockSpec((1, 2, n1, r, fw), lambda bi, i: (bi, 0, 0, i, 0)),
        out_shape=jax.ShapeDtypeStruct(ab.shape, BF16),
        compiler_params=_params(("parallel", "parallel"), VMEM_STREAMING),
        name="dft_stage1",
    )(m1, ab)


def _dft2_kernel(t_ref, y_ref, x_ref):
    xs = []
    for g in range(t_ref.shape[0]):
        yk = jnp.concatenate([y_ref[0, 0, g], y_ref[0, 1, g]], axis=0)
        xs.append(_dot(t_ref[g].astype(BF16), yk).astype(BF16))
    x_ref[0] = jnp.swapaxes(jnp.stack(xs, axis=0), 0, 1)


def _dft2_call(y, t):
    b, _, n1, n2, fw = y.shape
    g = min(n1, DFT2_ROWS // n2)
    return pl.pallas_call(
        _dft2_kernel,
        grid=(b, n1 // g),
        in_specs=[pl.BlockSpec((g, n2, 2 * n2), lambda bi, i: (i, 0, 0)),
                  pl.BlockSpec((1, 2, g, n2, fw), lambda bi, i: (bi, 0, i, 0, 0))],
        out_specs=pl.BlockSpec((1, n2, g, fw), lambda bi, i: (bi, 0, i, 0)),
        out_shape=jax.ShapeDtypeStruct((b, n2, n1, fw), BF16),
        compiler_params=_params(("parallel", "parallel"), VMEM_STREAMING),
        name="dft_stage2",
    )(t, y)


def _attn_kernel(qt_ref, k_ref, vt_ref, o_ref, qa_ref, m_ref, acc_ref):
    s_len = k_ref.shape[2]
    tq, tk = qa_ref.shape[1], ATTN_TK
    nblk = s_len // tk
    unroll = min(ATTN_UNROLL, nblk)

    def kv_block(j):
        koff = j * tk if isinstance(j, int) else pl.multiple_of(j * tk, tk)
        return k_ref[0, 0, pl.ds(koff, tk), :], vt_ref[0, 0, :, pl.ds(koff, tk)]

    ones = jnp.ones((16, QK_PAD), BF16)

    def key_norm(i, mx):
        for u in range(unroll):
            kb = kv_block(i * unroll + u)[0]
            mx = jnp.maximum(mx, _dot_nt(ones, kb * kb))
        return mx

    ksq = jnp.max(lax.fori_loop(0, nblk // unroll, key_norm, jnp.zeros((16, tk), F32)))
    row = lax.broadcasted_iota(jnp.int32, (QK_PAD, tq), 0)

    n_qt = s_len // tq
    trips_per_qt = nblk // unroll

    def tile_offset(qi):
        return qi * tq if isinstance(qi, int) else pl.multiple_of(qi * tq, tq)

    def write_tile(qi, acc):
        o_ref[0, :, pl.ds(tile_offset(qi), tq)] = (acc[:V_DIM] / acc[V_DIM:V_DIM + 1]).astype(BF16)

    def needs_exact(acc):
        return jnp.logical_not(jnp.min(acc[V_DIM:V_DIM + 1]) >= ATTN_DENOM_MIN)

    def fast_tile(qi):
        qf = qt_ref[0, 0, :, pl.ds(tile_offset(qi), tq)].astype(F32)
        bound = jnp.sqrt(jnp.sum(qf * qf, axis=0, keepdims=True) * ksq) * ATTN_BOUND_SLACK
        qa_ref[...] = jnp.where(row == ATTN_BIAS_ROW, -bound, qf).astype(BF16)

        def trip(t):
            part = None
            sc = _dot(kv_block(t * unroll)[0], qa_ref[...])
            for u in range(unroll):
                sc_next = _dot(kv_block(t * unroll + u + 1)[0], qa_ref[...]) if u + 1 < unroll else None
                pv = _dot(kv_block(t * unroll + u)[1], jnp.exp2(sc).astype(BF16))
                part = pv if part is None else part + pv
                sc = sc_next
            return part

        if trips_per_qt == 1:
            return trip(0)
        acc_ref[...] = jnp.zeros(acc_ref.shape, F32)

        def add_trip(t, c):
            acc_ref[...] += trip(t)
            return c

        lax.fori_loop(0, trips_per_qt, add_trip, 0)
        return acc_ref[...]

    def exact_tile(qi):
        qt = qt_ref[0, 0, :, pl.ds(tile_offset(qi), tq)]
        m_ref[...] = jnp.full(m_ref.shape, -1e30, F32)
        acc_ref[...] = jnp.zeros(acc_ref.shape, F32)

        def exact_block(j, c):
            kb, vb = kv_block(j)
            sc = _dot(kb, qt)
            m_old = m_ref[...]
            m_new = jnp.maximum(m_old, jnp.max(sc, axis=0, keepdims=True))
            acc_ref[...] = acc_ref[...] * jnp.exp2(m_old - m_new) + _dot(vb, jnp.exp2(sc - m_new).astype(BF16))
            m_ref[...] = m_new
            return c

        lax.fori_loop(0, nblk, exact_block, 0)
        write_tile(qi, acc_ref[...])

    def q_tile(qi, carry):
        acc = fast_tile(qi)
        write_tile(qi, acc)
        pl.when(needs_exact(acc))(functools.partial(exact_tile, qi))
        return carry

    lax.fori_loop(0, n_qt, q_tile, 0)


def _attn_call(qt, k, vt):
    b, h, _, s = qt.shape
    tq = ATTN_TQ
    return pl.pallas_call(
        _attn_kernel,
        grid=(b, h),
        in_specs=[pl.BlockSpec((1, 1, QK_PAD, s), lambda bi, hi: (bi, hi, 0, 0)),
                  pl.BlockSpec((1, 1, s, QK_PAD), lambda bi, hi: (bi, hi, 0, 0)),
                  pl.BlockSpec((1, 1, V_PAD, s), lambda bi, hi: (bi, hi, 0, 0))],
        out_specs=pl.BlockSpec((1, V_DIM, s), lambda bi, hi: (bi, hi, 0)),
        out_shape=jax.ShapeDtypeStruct((b, ATTN_WIDTH, s), BF16),
        scratch_shapes=[pltpu.VMEM((QK_PAD, tq), BF16), pltpu.VMEM((1, tq), F32), pltpu.VMEM((V_PAD, tq), F32)],
        compiler_params=_params(("parallel", "parallel"), VMEM_HEAD_RESIDENT),
        name="attention",
    )(qt, k, vt)


def _layer_norm(v, g, b):
    mu = jnp.mean(v, axis=-1, keepdims=True)
    var = jnp.mean(jnp.square(v - mu), axis=-1, keepdims=True)
    return (v - mu) * lax.rsqrt(var + LN_EPS) * g + b


def _final_kernel(alpha, x_ref, g_ref, f_ref, ot_ref, wf_ref, wa_ref, wo_ref, wgate_ref, wup_ref, wdown_ref,
                  ln1g_ref, ln1b_ref, ln2g_ref, ln2b_ref, y_ref):
    rows = x_ref.shape[1] // ROW_GROUPS
    groups = [slice(r * rows, (r + 1) * rows) for r in range(ROW_GROUPS)]
    mix = []
    for rs in groups:
        branch_a = _dot(f_ref[0, rs], wf_ref[...])
        branch_b = _dot_tn(ot_ref[0, :, rs], wa_ref[...])
        g = g_ref[0, rs]
        merged = (g[:, :D_MODEL].astype(F32) * branch_a + g[:, D_MODEL:].astype(F32) * branch_b).astype(BF16)
        mix.append(_dot(merged, wo_ref[...]))
    hs = [_layer_norm(alpha * x_ref[0, rs] + m, ln1g_ref[...], ln1b_ref[...]) for rs, m in zip(groups, mix)]
    acts = []
    for h in hs:
        hb = h.astype(BF16)
        acts.append((jax.nn.silu(_dot(hb, wgate_ref[...])) * _dot(hb, wup_ref[...])).astype(BF16))
    ffns = [_dot(a, wdown_ref[...]) for a in acts]
    for rs, h, ffn in zip(groups, hs, ffns):
        y_ref[0, rs] = _layer_norm(alpha * h + ffn, ln2g_ref[...], ln2b_ref[...])


def _final_call(x, gates, x3, ot, w, alpha):
    b, s, _ = x.shape
    tm = TOK_TILE
    consts = [w["wf"], w["wa"], w["wo"], w["wgate"], w["wup"], w["wdown"],
              w["ln1g"], w["ln1b"], w["ln2g"], w["ln2b"]]
    in_specs = [pl.BlockSpec((1, tm, D_MODEL), lambda bi, i: (bi, i, 0)),
                pl.BlockSpec((1, tm, 2 * D_MODEL), lambda bi, i: (bi, i, 0)),
                pl.BlockSpec((1, tm, FOURIER_WIDTH), lambda bi, i: (bi, i, 0)),
                pl.BlockSpec((1, ATTN_WIDTH, tm), lambda bi, i: (bi, 0, i))]
    in_specs += [_const_spec(c.shape) for c in consts]
    return pl.pallas_call(
        functools.partial(_final_kernel, alpha),
        grid=(b, s // tm),
        in_specs=in_specs,
        out_specs=pl.BlockSpec((1, tm, D_MODEL), lambda bi, i: (bi, i, 0)),
        out_shape=jax.ShapeDtypeStruct((b, s, D_MODEL), F32),
        compiler_params=_params(("parallel", "parallel"), VMEM_WEIGHT_RESIDENT),
        name="merge_ffn",
    )(x, gates, x3, ot, *consts)


def _cos_sin(num, den):
    ang = (2.0 * np.pi / den) * (num % den).astype(np.float64)
    return np.cos(ang), np.sin(ang)


def _channel_dft_table():
    j = np.arange(GROUP_DIM)
    c, s = _cos_sin(j[:, None] * j[None, :], GROUP_DIM)
    return (np.concatenate([c, s], axis=1) * GROUP_DIM ** -0.5).astype(np.float32)


def _dft_tables(s_len):
    n1, n2 = DFT_N1, s_len // DFT_N1
    i1 = np.arange(n1)
    c, s = _cos_sin(i1[:, None] * i1[None, :], n1)
    m1 = np.block([[c, -s], [-s, -c]]) * n1 ** -0.5
    i2 = np.arange(n2)
    k = i1[:, None, None] + n1 * i2[None, :, None]
    tc, ts = _cos_sin(k * i2[None, None, :], s_len)
    t = np.concatenate([tc, ts], axis=2) * n2 ** -0.5
    return m1.astype(np.float32), t.astype(np.float32)


def _rope_tables(s_len):
    f32 = np.float32
    inv_freq = 1.0 / ROPE_THETA ** (np.arange(0, ROPE_DIM, 2, dtype=np.float64) / ROPE_DIM)
    ang = np.arange(s_len, dtype=np.float64)[:, None] * inv_freq[None, :]
    cos2 = np.concatenate([np.cos(ang)] * 2, axis=1).astype(f32)
    sin2 = np.concatenate([np.sin(ang)] * 2, axis=1).astype(f32)
    pad = lambda t: np.pad(t, ((0, 0), (NOPE_DIM, QK_PAD - QK_DIM)))
    return {"cosq": np.ascontiguousarray(cos2.T), "sinq": np.ascontiguousarray(sin2.T),
            "cosk": pad(cos2), "sink": pad(sin2)}


def _rotate_half_cols(w):
    half = ROPE_DIM // 2
    return jnp.concatenate([-w[..., half:], w[..., :half]], axis=-1)


def _layer_weights(w_in, w_fourier, g_q, w_uq, g_kv, w_ukv, w_attn, w_o,
                   ln1_g, ln1_b, w_gate, w_up, w_down, ln2_g, ln2_b):
    o0 = FOURIER_WIDTH
    o1 = o0 + Q_RANK
    o2 = o1 + KV_RANK
    o3 = o2 + ROPE_DIM
    wkr = w_in[:, o2:o3]
    q_scale = (QK_DIM ** -0.5) * math.log2(math.e)
    uq = (w_uq * q_scale).reshape(Q_RANK, N_HEADS, QK_DIM)
    w1 = uq.reshape(Q_RANK, N_HEADS * QK_DIM).T
    ukv = w_ukv.reshape(KV_RANK, N_HEADS, NOPE_DIM + V_DIM)
    wuk = jnp.pad(ukv[:, :, :NOPE_DIM], ((0, 0), (0, 0), (0, QK_PAD - NOPE_DIM))).reshape(KV_RANK, N_HEADS * QK_PAD)
    wuvt = ukv[:, :, NOPE_DIM:].reshape(KV_RANK, ATTN_WIDTH).T
    row = lambda v: v.reshape(1, -1).astype(F32)
    return {
        "wab": _fold_call(w_in[:, :o0], _channel_dft_table()),
        "wlat": jnp.concatenate([w_in[:, o0:o2], _rotate_half_cols(wkr), jnp.zeros_like(wkr), wkr, jnp.zeros_like(wkr)],
                                axis=1).astype(BF16),
        "wg": w_in[:, o3:].astype(BF16), "gq": row(g_q), "gkv": row(g_kv),
        "w1": w1.astype(BF16), "wuk": wuk.astype(BF16), "wuvt": wuvt.astype(BF16),
        "wf": w_fourier.astype(BF16), "wa": w_attn.astype(BF16), "wo": w_o.astype(BF16),
        "wgate": w_gate.astype(BF16), "wup": w_up.astype(BF16), "wdown": w_down.astype(BF16),
        "ln1g": row(ln1_g), "ln1b": row(ln1_b), "ln2g": row(ln2_g), "ln2b": row(ln2_b),
    }


def _encoder_layer(x, w, alpha):
    b, s, _ = x.shape
    n1, n2 = DFT_N1, s // DFT_N1
    m1, t = _dft_tables(s)
    ab, qt, k, vt, gates = _proj_call(x, w, _rope_tables(s))
    y = _dft1_call(ab.reshape(b, 2, n1, n2, FOURIER_WIDTH), m1)
    x3 = _dft2_call(y, t)
    ot = _attn_call(qt, k, vt)
    return _final_call(x, gates, x3.reshape(b, s, FOURIER_WIDTH), ot, w, alpha)


def kernel(x_prompt, x_sample, w_in, w_fourier, g_q, w_uq, g_kv, w_ukv, w_attn, w_o, ln1_g, ln1_b, w_gate, w_up, w_down, ln2_g, ln2_b):
    stacked = (w_in, w_fourier, g_q, w_uq, g_kv, w_ukv, w_attn, w_o, ln1_g, ln1_b, w_gate, w_up, w_down, ln2_g, ln2_b)
    depth = w_in.shape[0]
    alpha = (2.0 * depth) ** 0.25
    layer = lambda t, l: t.reshape(t.shape[1:]) if depth == 1 else t[l]
    layers = [_layer_weights(*(layer(t, l) for t in stacked)) for l in range(depth)]
    outs = []
    for x in (x_prompt, x_sample):
        for w in layers:
            x = _encoder_layer(x, w, alpha)
        outs.append(x)
    return tuple(outs)
```
